```python
import jax, jax.numpy as jnp
from jax import lax
import numpy as np

D_MODEL = 2048
BATCH = 1
SEQ = 16384
DEPTH = 2

N_MEM = 256
EPS = 1e-6
NEG = -1e30
POOL_WINDOWS = (2, 4, 8, 16)
POOL_GROUPS = 4
POOL_GW = D_MODEL // 8
POOL_W = POOL_GROUPS * POOL_GW
SWA_CONFIGS = ((128, 1), (512, 4), (2048, 16))
SWA_GROUPS = 3
SWA_HPG = 4
SWA_HEADS = SWA_GROUPS * SWA_HPG
HEAD_DIM = 128
SWA_QKV = SWA_HEADS * HEAD_DIM
SWA_OUT = SWA_HPG * HEAD_DIM
BLK = 128
GLA_HEADS = 4
GLA_DK = D_MODEL // 16
GLA_DV = D_MODEL // 8
GLA_K = GLA_HEADS * GLA_DK
GLA_V = GLA_HEADS * GLA_DV
GLA_RANK = 16
GLA_TAU = 16.0
GLA_CHUNK = 64
MEM_HEADS = 4
MEM_W = MEM_HEADS * HEAD_DIM
N_BRANCH = 4
IN_SPLITS = (POOL_W, POOL_W,
             SWA_QKV, SWA_QKV, SWA_QKV, SWA_OUT,
             GLA_K, GLA_K, GLA_V, GLA_V, GLA_RANK,
             MEM_W, MEM_W,
             N_BRANCH * D_MODEL)
D_IN = sum(IN_SPLITS)

kernel_name = 'hybrid_pool_dilswa_gla_mem_block'


def rms_norm(x, g):
    xf = x.astype(jnp.float32)
    y = xf * lax.rsqrt(jnp.mean(xf * xf, axis=-1, keepdims=True) + EPS)
    return (y * g.astype(jnp.float32)).astype(x.dtype)


def pool_mixer(u, gate, w_pool, scale):
    B, S, _ = u.shape
    ug = u.reshape(B, S, POOL_GROUPS, POOL_GW).astype(jnp.float32)
    cs = jnp.cumsum(ug, axis=1)
    cs0 = jnp.pad(cs, ((0, 0), (1, 0), (0, 0), (0, 0)))
    pos = jnp.arange(S)
    outs = []
    for g, w in enumerate(POOL_WINDOWS):
        hi = cs[:, :, g]
        lo = jnp.pad(cs0[:, :S - w + 1, g], ((0, 0), (w - 1, 0), (0, 0)))
        cnt = jnp.minimum(pos + 1, w).astype(jnp.float32)[None, :, None]
        outs.append((hi - lo) / cnt - ug[:, :, g])
    pooled = jnp.stack(outs, axis=2).astype(u.dtype)
    mixed = jnp.einsum('bsgc,gcd->bsgd', pooled, w_pool).reshape(B, S, POOL_W) * scale
    return mixed * jax.nn.silu(gate)


def dilated_window_attn(q, k, v, window, dilation, slopes):
    B, S, H, Dh = q.shape
    nk = window // dilation
    L = -(-S // (dilation * BLK)) * BLK
    pad = L * dilation - S
    nb = L // BLK

    def to_blocks(t):
        t = jnp.pad(t, ((0, 0), (0, pad), (0, 0), (0, 0)))
        t = t.reshape(B, L, dilation, H, Dh).transpose(0, 2, 1, 3, 4)
        return t.reshape(B, dilation, nb, BLK, H, Dh)

    def band(t):
        prev = jnp.pad(t[:, :, :-1], ((0, 0), (0, 0), (1, 0), (0, 0), (0, 0), (0, 0)))
        return jnp.concatenate([prev, t], axis=3)

    qb = to_blocks(q)
    kk = band(to_blocks(k))
    vv = band(to_blocks(v))
    s = jnp.einsum('brnqhd,brnkhd->brnhqk', qb, kk).astype(jnp.float32) * (Dh ** -0.5)
    qi = jnp.arange(BLK)[:, None]
    kj = jnp.arange(2 * BLK)[None, :] - BLK
    delta = qi - kj
    first = (jnp.arange(nb) == 0)[:, None, None]
    valid = (delta >= 0) & (delta <= nk) & ~(first & (kj < 0))
    bias = -slopes[:, None, None] * (delta * dilation).astype(jnp.float32)
    s = jnp.where(valid[None, None, :, None], s + bias[None, None, None], NEG)
    m = jnp.max(s, axis=-1, keepdims=True)
    e = jnp.exp(s - m)
    den = jnp.sum(e, axis=-1, keepdims=True)
    o = jnp.einsum('brnhqk,brnkhd->brnqhd', (e / den).astype(v.dtype), vv)
    lse = (m + jnp.log(den))[..., 0]
    o = o.reshape(B, dilation, L, H, Dh).transpose(0, 2, 1, 3, 4).reshape(B, L * dilation, H, Dh)[:, :S]
    lse = lse.transpose(0, 1, 2, 4, 3).reshape(B, dilation, L, H).transpose(0, 2, 1, 3)
    lse = lse.reshape(B, L * dilation, H)[:, :S]
    return o, lse


def swa_mixer(q, k, v, gate):
    B, S, _ = q.shape
    slopes = jnp.exp2(-8.0 * (jnp.arange(SWA_HEADS, dtype=jnp.float32) + 1.0) / SWA_HEADS)
    qg = q.reshape(B, S, SWA_GROUPS, SWA_HPG, HEAD_DIM)
    kg = k.reshape(B, S, SWA_GROUPS, SWA_HPG, HEAD_DIM)
    vg = v.reshape(B, S, SWA_GROUPS, SWA_HPG, HEAD_DIM)
    outs, lses = [], []
    for g, (w, d) in enumerate(SWA_CONFIGS):
        o_g, l_g = dilated_window_attn(qg[:, :, g], kg[:, :, g], vg[:, :, g], w, d,
                                       slopes[g * SWA_HPG:(g + 1) * SWA_HPG])
        outs.append(o_g)
        lses.append(l_g)
    o = jnp.stack(outs, axis=2)
    wts = jax.nn.softmax(jnp.stack(lses, axis=2), axis=2)
    o = jnp.sum(wts[..., None].astype(o.dtype) * o, axis=2).reshape(B, S, SWA_OUT)
    return o * jax.nn.silu(gate)


def gla_mixer(q, k, v, lr, gate, w_alpha, b_alpha, g_gla):
    B, S, _ = q.shape
    C = GLA_CHUNK
    nc = S // C
    f32 = jnp.float32
    z = (lr @ w_alpha + b_alpha).astype(f32)
    log_a = jax.nn.log_sigmoid(z) / GLA_TAU
    qc = q.astype(f32).reshape(B, nc, C, GLA_HEADS, GLA_DK) * (GLA_DK ** -0.5)
    kc = k.astype(f32).reshape(B, nc, C, GLA_HEADS, GLA_DK)
    vc = v.astype(f32).reshape(B, nc, C, GLA_HEADS, GLA_DV)
    bc = jnp.cumsum(log_a.reshape(B, nc, C, GLA_HEADS, GLA_DK), axis=2)
    b_last = bc[:, :, -1]
    q_t = qc * jnp.exp(bc)
    k_t = kc * jnp.exp(-bc)
    causal = jnp.tril(jnp.ones((C, C), dtype=bool))
    a = jnp.where(causal, jnp.einsum('bcihk,bcjhk->bchij', q_t, k_t), 0.0)
    o_intra = jnp.einsum('bchij,bcjhv->bcihv', a, vc)
    kv = jnp.einsum('bcjhk,bcjhv->bchkv', kc * jnp.exp(b_last[:, :, None] - bc), vc)
    decay = jnp.exp(b_last)

    def step(state, inp):
        dec, kv_c = inp
        return dec[..., None] * state + kv_c, state

    _, s_prev = lax.scan(step, jnp.zeros((B, GLA_HEADS, GLA_DK, GLA_DV), f32),
                         (decay.transpose(1, 0, 2, 3), kv.transpose(1, 0, 2, 3, 4)))
    s_prev = s_prev.transpose(1, 0, 2, 3, 4)
    o = o_intra + jnp.einsum('bcihk,bchkv->bcihv', q_t, s_prev)
    o = o.reshape(B, S, GLA_HEADS, GLA_DV)
    o = o * lax.rsqrt(jnp.mean(o * o, axis=-1, keepdims=True) + EPS)
    o = o.reshape(B, S, GLA_V) * g_gla.astype(f32)
    return o.astype(q.dtype) * jax.nn.silu(gate)


def mem_attn(q, gate, mem, g_mem, w_mem_kv):
    B, S, _ = q.shape
    kv = rms_norm(mem, g_mem) @ w_mem_kv
    km, vm = jnp.split(kv, 2, axis=-1)
    km = km.reshape(B, N_MEM, MEM_HEADS, HEAD_DIM)
    vm = vm.reshape(B, N_MEM, MEM_HEADS, HEAD_DIM)
    qh = q.reshape(B, S, MEM_HEADS, HEAD_DIM)
    s = jnp.einsum('bshd,bmhd->bhsm', qh, km).astype(jnp.float32) * (HEAD_DIM ** -0.5)
    p = jax.nn.softmax(s, axis=-1)
    o = jnp.einsum('bhsm,bmhd->bshd', p.astype(vm.dtype), vm).reshape(B, S, MEM_W)
    return o * jax.nn.silu(gate)


def setup_inputs(seed: int = 0) -> dict:
    key = jax.random.key(seed)
    ks = jax.random.split(key, 20)
    f32 = jnp.float32
    nrm = lambda k, shape, scale: (jax.random.normal(k, shape, f32) * scale)
    return {
        'x': nrm(ks[0], (BATCH, SEQ, D_MODEL), 1.0),
        'mem': nrm(ks[1], (BATCH, N_MEM, D_MODEL), 1.0),
        'g_pre': 1.0 + nrm(ks[2], (DEPTH, D_MODEL), 0.05),
        'g_post': 1.0 + nrm(ks[3], (DEPTH, D_MODEL), 0.05),
        'g_mem': 1.0 + nrm(ks[4], (DEPTH, D_MODEL), 0.05),
        'w_in': nrm(ks[5], (DEPTH, D_MODEL, D_IN), D_MODEL ** -0.5),
        'b_merge': nrm(ks[6], (DEPTH, N_BRANCH, D_MODEL), 0.1),
        'w_pool': nrm(ks[7], (DEPTH, POOL_GROUPS, POOL_GW, POOL_GW), POOL_GW ** -0.5),
        'pool_scale': 1.0 + nrm(ks[8], (DEPTH, POOL_W), 0.05),
        'w_alpha': nrm(ks[9], (DEPTH, GLA_RANK, GLA_K), GLA_RANK ** -0.5),
        'b_alpha': nrm(ks[10], (DEPTH, GLA_K), 0.1),
        'g_gla': 1.0 + nrm(ks[11], (DEPTH, GLA_V), 0.05),
        'w_mem_kv': nrm(ks[12], (DEPTH, D_MODEL, 2 * MEM_W), D_MODEL ** -0.5),
        'w_br_pool': nrm(ks[13], (DEPTH, POOL_W, D_MODEL), POOL_W ** -0.5),
        'w_br_swa': nrm(ks[14], (DEPTH, SWA_OUT, D_MODEL), SWA_OUT ** -0.5),
        'w_br_gla': nrm(ks[15], (DEPTH, GLA_V, D_MODEL), GLA_V ** -0.5),
        'w_br_mem': nrm(ks[16], (DEPTH, MEM_W, D_MODEL), MEM_W ** -0.5),
        'w_out': nrm(ks[17], (DEPTH, D_MODEL, D_MODEL), D_MODEL ** -0.5),
    }


def reference(x, mem, g_pre, g_post, g_mem, w_in, b_merge, w_pool, pool_scale, w_alpha, b_alpha,
              g_gla, w_mem_kv, w_br_pool, w_br_swa, w_br_gla, w_br_mem, w_out):
    B, S, _ = x.shape
    split_at = np.cumsum(IN_SPLITS)[:-1]
    for l in range(DEPTH):
        h = rms_norm(x, g_pre[l])
        proj = h @ w_in[l]
        (a_val, a_gate, sq, sk, sv, s_gate, cq, ck, cv, c_gate, c_lr,
         mq, m_gate, g_logits) = jnp.split(proj, split_at, axis=-1)
        y_a = pool_mixer(a_val, a_gate, w_pool[l], pool_scale[l])
        y_b = swa_mixer(sq, sk, sv, s_gate)
        y_c = gla_mixer(cq, ck, cv, c_lr, c_gate, w_alpha[l], b_alpha[l], g_gla[l])
        y_m = mem_attn(mq, m_gate, mem, g_mem[l], w_mem_kv[l])
        gates = jax.nn.sigmoid(g_logits.reshape(B, S, N_BRANCH, D_MODEL) + b_merge[l])
        merged = (gates[:, :, 0] * (y_a @ w_br_pool[l]) + gates[:, :, 1] * (y_b @ w_br_swa[l])
                  + gates[:, :, 2] * (y_c @ w_br_gla[l]) + gates[:, :, 3] * (y_m @ w_br_mem[l]))
        x = x + rms_norm(merged @ w_out[l], g_post[l])
    return x
```

```python
import functools

import jax
import jax.numpy as jnp
from jax import lax
from jax.experimental import pallas as pl
from jax.experimental.pallas import tpu as pltpu

F32 = jnp.float32
BF16 = jnp.bfloat16

D_MODEL = 2048
SEQ = 16384
N_MEM = 256
EPS = 1e-6
NEG = -1e30
POOL_WINDOWS = (2, 4, 8, 16)
POOL_GW = 256
POOL_W = 1024
POOL_HALO = 16
SWA_CONFIGS = ((128, 1), (512, 4), (2048, 16))
SWA_HPG = 4
SWA_HEADS = 12
HEAD_DIM = 128
SWA_OUT = 512
BLK = 128
GLA_HEADS = 4
GLA_DK = 128
GLA_DV = 256
GLA_K = 512
GLA_V = 1024
GLA_RANK = 16
GLA_TAU = 16.0
GLA_CHUNK = 64
MEM_HEADS = 4
MEM_W = 512
N_BRANCH = 4
LANE = 128

OFF_A_VAL, OFF_A_GATE = 0, 1024
OFF_SQ, OFF_SK, OFF_SV, OFF_SGATE = 2048, 3584, 5120, 6656
OFF_CQ, OFF_CK, OFF_CV, OFF_CGATE, OFF_CLR = 7168, 7680, 8192, 9216, 10240
OFF_MQ, OFF_MGATE, OFF_GLOG = 10256, 10768, 11280

MAIN_W = 8192
CB_AVAL, CB_AGATE = 0, 2
CB_Q0, CB_K0, CB_V0, CB_SGATE = 4, 5, 6, 7
CB_CQ, CB_CK, CB_CV, CB_CGATE = 8, 9, 10, 12
CB_MQ, CB_MGATE = 14, 15

MIB = 1024 * 1024


def _params(sem, vmem_mib):
    return pltpu.CompilerParams(dimension_semantics=sem, vmem_limit_bytes=vmem_mib * MIB)


def _sigmoid(x):
    return 1.0 / (1.0 + jnp.exp(-x))


def _silu(x):
    return x * _sigmoid(x)


def _rmsnorm_kernel(x_ref, g_ref, o_ref):
    x = x_ref[...]
    ms = jnp.mean(x * x, axis=-1, keepdims=True)
    o_ref[...] = (x * lax.rsqrt(ms + EPS) * g_ref[...]).astype(o_ref.dtype)


def _rmsnorm(x, g, tm=512):
    s, d = x.shape
    return pl.pallas_call(
        _rmsnorm_kernel,
        out_shape=jax.ShapeDtypeStruct((s, d), BF16),
        grid=(s // tm,),
        in_specs=[pl.BlockSpec((tm, d), lambda i: (i, 0)),
                  pl.BlockSpec((1, d), lambda i: (0, 0))],
        out_specs=pl.BlockSpec((tm, d), lambda i: (i, 0)),
        compiler_params=_params(("parallel",), 32),
        name="rmsnorm",
    )(x, g.reshape(1, d))


def _matmul_kernel(h_ref, w_ref, o_ref):
    o_ref[...] = jnp.dot(h_ref[...], w_ref[...], preferred_element_type=F32).astype(o_ref.dtype)


def _project(h, w, *, dil=1, tm=1024, tn=1024, name="project"):
    s, d = h.shape
    n = w.shape[1]
    sub = s // dil
    tm = min(tm, sub)
    tn = min(tn, n)
    mt = sub // tm
    h_view = h.reshape(sub, dil * d)
    return pl.pallas_call(
        _matmul_kernel,
        out_shape=jax.ShapeDtypeStruct((s, n), BF16),
        grid=(dil, mt, n // tn),
        in_specs=[pl.BlockSpec((tm, d), lambda r, i, j: (i, r)),
                  pl.BlockSpec((d, tn), lambda r, i, j: (0, j))],
        out_specs=pl.BlockSpec((tm, tn), lambda r, i, j: (r * mt + i, j)),
        compiler_params=_params(("parallel", "parallel", "arbitrary"), 48),
        name=name,
    )(h_view, w)


def _pool_kernel(u_ref, halo_ref, gate_ref, w_ref, sc_ref, o_ref, *, tp):
    i = pl.program_id(0)
    u = u_ref[...].astype(F32)
    halo = jnp.where(i == 0, 0.0, halo_ref[...].astype(F32))
    full = jnp.concatenate([halo, u], axis=0)
    pos = i * tp + lax.broadcasted_iota(jnp.int32, (tp, 1), 0)
    for g, w in enumerate(POOL_WINDOWS):
        cols = slice(g * POOL_GW, (g + 1) * POOL_GW)
        s = full[:, cols]
        k = 1
        while k < w:
            s = s + pltpu.roll(s, k, axis=0)
            k *= 2
        s = s[POOL_HALO:]
        cnt = jnp.minimum(pos + 1, w).astype(F32)
        pooled = s / cnt - u[:, cols]
        mixed = jnp.dot(pooled.astype(BF16), w_ref[g], preferred_element_type=F32) * sc_ref[:, cols]
        gate = gate_ref[:, cols].astype(F32)
        o_ref[:, cols] = (mixed * _silu(gate)).astype(o_ref.dtype)


def _pool_mixer(main, w_pool, pool_scale, tp=512):
    s = main.shape[0]
    hb = tp // POOL_HALO
    return pl.pallas_call(
        functools.partial(_pool_kernel, tp=tp),
        out_shape=jax.ShapeDtypeStruct((s, POOL_W), BF16),
        grid=(s // tp,),
        in_specs=[pl.BlockSpec((tp, POOL_W), lambda i: (i, CB_AVAL // 2)),
                  pl.BlockSpec((POOL_HALO, POOL_W), lambda i: (jnp.maximum(i * hb - 1, 0), CB_AVAL // 2)),
                  pl.BlockSpec((tp, POOL_W), lambda i: (i, CB_AGATE // 2)),
                  pl.BlockSpec((4, POOL_GW, POOL_GW), lambda i: (0, 0, 0)),
                  pl.BlockSpec((1, POOL_W), lambda i: (0, 0))],
        out_specs=pl.BlockSpec((tp, POOL_W), lambda i: (i, 0)),
        compiler_params=_params(("parallel",), 32),
        name="pool_mixer",
    )(main, main, main, w_pool, pool_scale.reshape(1, POOL_W))


def _swa_kernel(q_ref, kc_ref, kp_ref, vc_ref, vp_ref, o_ref, lse_ref, *, dil, slopes, tq):
    n = pl.program_id(1)
    qi = lax.broadcasted_iota(jnp.int32, (BLK, 2 * BLK), 0)
    kj = lax.broadcasted_iota(jnp.int32, (BLK, 2 * BLK), 1)
    delta = qi + BLK - kj
    in_window = (delta >= 0) & (delta <= BLK)
    not_before_start = jnp.logical_not((n == 0) & (kj < BLK))
    dist = (delta * dil).astype(F32)
    lane = lax.broadcasted_iota(jnp.int32, (BLK, LANE), 1)
    scale = HEAD_DIM ** -0.5
    for b in range(tq // BLK):
        rows = slice(b * BLK, (b + 1) * BLK)
        if b == 0:
            k_blk = jnp.concatenate([kp_ref[...], kc_ref[0:BLK, :]], axis=0)
            v_blk = jnp.concatenate([vp_ref[...], vc_ref[0:BLK, :]], axis=0)
            valid = in_window & not_before_start
        else:
            k_blk = kc_ref[(b - 1) * BLK:(b + 1) * BLK, :]
            v_blk = vc_ref[(b - 1) * BLK:(b + 1) * BLK, :]
            valid = in_window
        q_blk = q_ref[rows, :]
        lse_acc = jnp.zeros((BLK, LANE), F32)
        for h in range(SWA_HPG):
            hc = slice(h * HEAD_DIM, (h + 1) * HEAD_DIM)
            s = lax.dot_general(q_blk[:, hc], k_blk[:, hc], (((1,), (1,)), ((), ())),
                                preferred_element_type=F32) * scale
            s = jnp.where(valid, s - slopes[h] * dist, NEG)
            m = jnp.max(s, axis=-1, keepdims=True)
            e = jnp.exp(s - m)
            den = jnp.sum(e, axis=-1, keepdims=True)
            p = (e / den).astype(BF16)
            o_ref[rows, hc] = jnp.dot(p, v_blk[:, hc], preferred_element_type=F32)
            lse_acc = jnp.where(lane == h, m + jnp.log(den), lse_acc)
        lse_ref[rows, :] = lse_acc


def _swa_group(qkv, cols, group, tq=512):
    window, dil = SWA_CONFIGS[group]
    s = qkv.shape[0]
    sub = s // dil
    tq = min(tq, sub)
    nq = sub // tq
    pb = tq // BLK
    qc, kc, vc = cols
    slopes = tuple(float(2.0 ** (-8.0 * (group * SWA_HPG + h + 1.0) / SWA_HEADS)) for h in range(SWA_HPG))
    cur = lambda c: pl.BlockSpec((tq, SWA_OUT), lambda r, n: (r * nq + n, c))
    prev = lambda c: pl.BlockSpec((BLK, SWA_OUT), lambda r, n: (jnp.maximum((r * nq + n) * pb - 1, 0), c))
    o, lse = pl.pallas_call(
        functools.partial(_swa_kernel, dil=dil, slopes=slopes, tq=tq),
        out_shape=(jax.ShapeDtypeStruct((sub, dil * SWA_OUT), F32),
                   jax.ShapeDtypeStruct((sub, dil * LANE), F32)),
        grid=(dil, nq),
        in_specs=[cur(qc), cur(kc), prev(kc), cur(vc), prev(vc)],
        out_specs=(pl.BlockSpec((tq, SWA_OUT), lambda r, n: (n, r)),
                   pl.BlockSpec((tq, LANE), lambda r, n: (n, r))),
        compiler_params=_params(("parallel", "parallel"), 32),
        name=f"swa_group{group}",
    )(qkv, qkv, qkv, qkv, qkv)
    return o.reshape(s, SWA_OUT), lse.reshape(s, LANE)


def _swa_combine_kernel(o0, o1, o2, l0, l1, l2, gate_ref, y_ref):
    ls = [l0[...], l1[...], l2[...]]
    mx = jnp.maximum(jnp.maximum(ls[0], ls[1]), ls[2])
    es = [jnp.exp(l - mx) for l in ls]
    den = es[0] + es[1] + es[2]
    ws = [e / den for e in es]
    os_ = [o0, o1, o2]
    for h in range(SWA_HPG):
        hc = slice(h * HEAD_DIM, (h + 1) * HEAD_DIM)
        acc = None
        for g in range(3):
            term = ws[g][:, h:h + 1] * os_[g][:, hc]
            acc = term if acc is None else acc + term
        gate = gate_ref[:, hc].astype(F32)
        y_ref[:, hc] = (acc * _silu(gate)).astype(y_ref.dtype)


def _swa_combine(os_, lses, main, tm=512):
    s = main.shape[0]
    ospec = pl.BlockSpec((tm, SWA_OUT), lambda i: (i, 0))
    lspec = pl.BlockSpec((tm, LANE), lambda i: (i, 0))
    return pl.pallas_call(
        _swa_combine_kernel,
        out_shape=jax.ShapeDtypeStruct((s, SWA_OUT), BF16),
        grid=(s // tm,),
        in_specs=[ospec, ospec, ospec, lspec, lspec, lspec,
                  pl.BlockSpec((tm, SWA_OUT), lambda i: (i, CB_SGATE))],
        out_specs=ospec,
        compiler_params=_params(("parallel",), 32),
        name="swa_combine",
    )(*os_, *lses, main)


def _gla_kernel(q_ref, k_ref, v_ref, gate_ref, lr_ref, wa_ref, ba_ref, gg_ref, o_ref, state_ref, *, tt):
    @pl.when(pl.program_id(0) == 0)
    def _():
        state_ref[...] = jnp.zeros_like(state_ref)

    c = GLA_CHUNK
    z = jnp.dot(lr_ref[...], wa_ref[...], preferred_element_type=F32) + ba_ref[...]
    log_a = (jnp.minimum(z, 0.0) - jnp.log1p(jnp.exp(-jnp.abs(z)))) / GLA_TAU
    in_chunk = lax.broadcasted_iota(jnp.int32, (tt, GLA_K), 0) & (c - 1)
    bc = log_a
    k = 1
    while k < c:
        bc = bc + jnp.where(in_chunk >= k, pltpu.roll(bc, k, axis=0), 0.0)
        k *= 2
    q_t = (q_ref[...].astype(F32) * (GLA_DK ** -0.5)) * jnp.exp(bc)
    kf = k_ref[...].astype(F32)
    k_t = kf * jnp.exp(-bc)
    q_tb = q_t.astype(BF16)
    k_tb = k_t.astype(BF16)
    ri = lax.broadcasted_iota(jnp.int32, (tt, tt), 0)
    ci = lax.broadcasted_iota(jnp.int32, (tt, tt), 1)
    causal = ((ri & -c) == (ci & -c)) & (ci <= ri)
    nchunk = tt // c
    b_last = [bc[(j + 1) * c - 1:(j + 1) * c, :] for j in range(nchunk)]
    k_dec = jnp.concatenate(
        [kf[j * c:(j + 1) * c, :] * jnp.exp(b_last[j] - bc[j * c:(j + 1) * c, :]) for j in range(nchunk)],
        axis=0).astype(BF16)
    decay = [jnp.exp(b) for b in b_last]
    for h in range(GLA_HEADS):
        kc = slice(h * GLA_DK, (h + 1) * GLA_DK)
        vc = slice(h * GLA_DV, (h + 1) * GLA_DV)
        v_h = v_ref[:, vc]
        a = lax.dot_general(q_tb[:, kc], k_tb[:, kc], (((1,), (1,)), ((), ())), preferred_element_type=F32)
        a = jnp.where(causal, a, 0.0)
        o = jnp.dot(a.astype(BF16), v_h, preferred_element_type=F32)
        state = state_ref[h]
        inter = []
        for j in range(nchunk):
            rows = slice(j * c, (j + 1) * c)
            inter.append(lax.dot_general(q_tb[rows, kc], state.astype(BF16), (((1,), (1,)), ((), ())),
                                         preferred_element_type=F32))
            kv_t = lax.dot_general(v_h[rows, :], k_dec[rows, kc], (((0,), (0,)), ((), ())),
                                   preferred_element_type=F32)
            state = decay[j][:, kc] * state + kv_t
        state_ref[h] = state
        o = o + jnp.concatenate(inter, axis=0)
        o = o * lax.rsqrt(jnp.mean(o * o, axis=-1, keepdims=True) + EPS)
        o = o * gg_ref[:, vc]
        gate = gate_ref[:, vc].astype(F32)
        o_ref[:, vc] = (o * _silu(gate)).astype(o_ref.dtype)


def _gla_mixer(main, lr, w_alpha_pad, b_alpha, g_gla, tt=256):
    s = main.shape[0]
    return pl.pallas_call(
        functools.partial(_gla_kernel, tt=tt),
        out_shape=jax.ShapeDtypeStruct((s, GLA_V), BF16),
        grid=(s // tt,),
        in_specs=[pl.BlockSpec((tt, GLA_K), lambda i: (i, CB_CQ)),
                  pl.BlockSpec((tt, GLA_K), lambda i: (i, CB_CK)),
                  pl.BlockSpec((tt, GLA_V), lambda i: (i, CB_CV // 2)),
                  pl.BlockSpec((tt, GLA_V), lambda i: (i, CB_CGATE // 2)),
                  pl.BlockSpec((tt, LANE), lambda i: (i, 0)),
                  pl.BlockSpec((LANE, GLA_K), lambda i: (0, 0)),
                  pl.BlockSpec((1, GLA_K), lambda i: (0, 0)),
                  pl.BlockSpec((1, GLA_V), lambda i: (0, 0))],
        out_specs=pl.BlockSpec((tt, GLA_V), lambda i: (i, 0)),
        scratch_shapes=[pltpu.VMEM((GLA_HEADS, GLA_DV, GLA_DK), F32)],
        compiler_params=_params(("arbitrary",), 32),
        name="gla_mixer",
    )(main, main, main, main, lr, w_alpha_pad, b_alpha.reshape(1, GLA_K), g_gla.reshape(1, GLA_V))


def _mem_kv_kernel(mem_ref, g_ref, w_ref, o_ref):
    x = mem_ref[...]
    ms = jnp.mean(x * x, axis=-1, keepdims=True)
    hm = (x * lax.rsqrt(ms + EPS) * g_ref[...]).astype(BF16)
    o_ref[...] = jnp.dot(hm, w_ref[...], preferred_element_type=F32).astype(o_ref.dtype)


def _mem_kv(mem, g_mem, w_mem_kv):
    return pl.pallas_call(
        _mem_kv_kernel,
        out_shape=jax.ShapeDtypeStruct((N_MEM, 2 * MEM_W), BF16),
        compiler_params=pltpu.CompilerParams(vmem_limit_bytes=32 * MIB),
        name="mem_kv",
    )(mem, g_mem.reshape(1, D_MODEL), w_mem_kv)


def _mem_attn_kernel(q_ref, gate_ref, kv_ref, o_ref):
    scale = HEAD_DIM ** -0.5
    for h in range(MEM_HEADS):
        hc = slice(h * HEAD_DIM, (h + 1) * HEAD_DIM)
        km = kv_ref[:, h * HEAD_DIM:(h + 1) * HEAD_DIM]
        vm = kv_ref[:, MEM_W + h * HEAD_DIM:MEM_W + (h + 1) * HEAD_DIM]
        s = lax.dot_general(q_ref[:, hc], km, (((1,), (1,)), ((), ())), preferred_element_type=F32) * scale
        m = jnp.max(s, axis=-1, keepdims=True)
        e = jnp.exp(s - m)
        p = (e / jnp.sum(e, axis=-1, keepdims=True)).astype(BF16)
        o = jnp.dot(p, vm, preferred_element_type=F32)
        gate = gate_ref[:, hc].astype(F32)
        o_ref[:, hc] = (o * _silu(gate)).astype(o_ref.dtype)


def _mem_attn(main, kv, tm=512):
    s = main.shape[0]
    return pl.pallas_call(
        _mem_attn_kernel,
        out_shape=jax.ShapeDtypeStruct((s, MEM_W), BF16),
        grid=(s // tm,),
        in_specs=[pl.BlockSpec((tm, MEM_W), lambda i: (i, CB_MQ)),
                  pl.BlockSpec((tm, MEM_W), lambda i: (i, CB_MGATE)),
                  pl.BlockSpec((N_MEM, 2 * MEM_W), lambda i: (0, 0))],
        out_specs=pl.BlockSpec((tm, MEM_W), lambda i: (i, 0)),
        compiler_params=_params(("parallel",), 32),
        name="mem_attn",
    )(main, main, kv)


def _merge_kernel(ya_ref, yb_ref, yc_ref, ym_ref, gl_ref, bm_ref, wa_ref, wb_ref, wc_ref, wm_ref, wo_ref,
                  gp_ref, x_ref, o_ref):
    merged = None
    for j, (y_ref, w_ref) in enumerate(((ya_ref, wa_ref), (yb_ref, wb_ref), (yc_ref, wc_ref), (ym_ref, wm_ref))):
        br = jnp.dot(y_ref[...], w_ref[...], preferred_element_type=F32)
        gate = _sigmoid(gl_ref[:, j * D_MODEL:(j + 1) * D_MODEL].astype(F32) + bm_ref[j:j + 1, :])
        term = gate * br
        merged = term if merged is None else merged + term
    out = jnp.dot(merged.astype(BF16), wo_ref[...], preferred_element_type=F32)
    ms = jnp.mean(out * out, axis=-1, keepdims=True)
    o_ref[...] = x_ref[...] + out * lax.rsqrt(ms + EPS) * gp_ref[...]


def _merge(ya, yb, yc, ym, glog, b_merge, wa, wb, wc, wm, wo, g_post, x, tm=256):
    s = x.shape[0]
    row = lambda w: pl.BlockSpec((tm, w), lambda i: (i, 0))
    const = lambda a: pl.BlockSpec(a.shape, lambda i: (0,) * a.ndim, pipeline_mode=pl.Buffered(1))
    g_post = g_post.reshape(1, D_MODEL)
    return pl.pallas_call(
        _merge_kernel,
        out_shape=jax.ShapeDtypeStruct((s, D_MODEL), F32),
        grid=(s // tm,),
        in_specs=[row(POOL_W), row(SWA_OUT), row(GLA_V), row(MEM_W), row(N_BRANCH * D_MODEL),
                  const(b_merge), const(wa), const(wb), const(wc), const(wm), const(wo), const(g_post),
                  row(D_MODEL)],
        out_specs=row(D_MODEL),
        compiler_params=_params(("parallel",), 56),
        name="merge_out",
    )(ya, yb, yc, ym, glog, b_merge, wa, wb, wc, wm, wo, g_post, x)


def _layer(x, mem, g_pre, g_post, g_mem, w_in, b_merge, w_pool, pool_scale, w_alpha, b_alpha, g_gla,
           w_mem_kv, w_br_pool, w_br_swa, w_br_gla, w_br_mem, w_out):
    def cols(off, width):
        return w_in[:, off:off + width]

    def swa_cols(off, group):
        return cols(off + group * SWA_OUT, SWA_OUT)

    w_main = jnp.concatenate([
        cols(OFF_A_VAL, 2 * POOL_W),
        swa_cols(OFF_SQ, 0), swa_cols(OFF_SK, 0), swa_cols(OFF_SV, 0), cols(OFF_SGATE, SWA_OUT),
        cols(OFF_CQ, 2 * GLA_K + 2 * GLA_V),
        cols(OFF_MQ, 2 * MEM_W)], axis=1).astype(BF16)
    w_lr = jnp.pad(cols(OFF_CLR, GLA_RANK), ((0, 0), (0, LANE - GLA_RANK))).astype(BF16)
    w_glog = cols(OFF_GLOG, N_BRANCH * D_MODEL).astype(BF16)
    w_dil = [jnp.concatenate([swa_cols(OFF_SQ, g), swa_cols(OFF_SK, g), swa_cols(OFF_SV, g)], axis=1).astype(BF16)
             for g in (1, 2)]
    w_alpha_pad = jnp.pad(w_alpha, ((0, LANE - GLA_RANK), (0, 0))).astype(BF16)

    h = _rmsnorm(x, g_pre)
    main = _project(h, w_main, name="project_main")
    lr = _project(h, w_lr, name="project_lr")
    glog = _project(h, w_glog, name="project_gates")
    qkv1 = _project(h, w_dil[0], dil=SWA_CONFIGS[1][1], tn=512, name="project_dil4")
    qkv2 = _project(h, w_dil[1], dil=SWA_CONFIGS[2][1], tn=512, name="project_dil16")

    ya = _pool_mixer(main, w_pool.astype(BF16), pool_scale)
    o0, l0 = _swa_group(main, (CB_Q0, CB_K0, CB_V0), 0)
    o1, l1 = _swa_group(qkv1, (0, 1, 2), 1)
    o2, l2 = _swa_group(qkv2, (0, 1, 2), 2)
    yb = _swa_combine((o0, o1, o2), (l0, l1, l2), main)
    yc = _gla_mixer(main, lr, w_alpha_pad, b_alpha, g_gla)
    kv = _mem_kv(mem, g_mem, w_mem_kv.astype(BF16))
    ym = _mem_attn(main, kv)
    return _merge(ya, yb, yc, ym, glog, b_merge, w_br_pool.astype(BF16), w_br_swa.astype(BF16),
                  w_br_gla.astype(BF16), w_br_mem.astype(BF16), w_out.astype(BF16), g_post, x)


def kernel(x, mem, g_pre, g_post, g_mem, w_in, b_merge, w_pool, pool_scale, w_alpha, b_alpha, g_gla, w_mem_kv,
           w_br_pool, w_br_swa, w_br_gla, w_br_mem, w_out):
    b, s, d = x.shape
    assert (b, s, d) == (1, SEQ, D_MODEL) and mem.shape == (1, N_MEM, D_MODEL)
    xs = x[0]
    for l in range(g_pre.shape[0]):
        xs = _layer(xs, mem[0], g_pre[l], g_post[l], g_mem[l], w_in[l], b_merge[l], w_pool[l], pool_scale[l],
                    w_alpha[l], b_alpha[l], g_gla[l], w_mem_kv[l], w_br_pool[l], w_br_swa[l], w_br_gla[l],
                    w_br_mem[l], w_out[l])
    return xs[None]
```

```python
import functools

import jax
import jax.numpy as jnp
from jax import lax
from jax.experimental import pallas as pl
from jax.experimental.pallas import tpu as pltpu

F32 = jnp.float32
BF16 = jnp.bfloat16

D_MODEL = 2048
SEQ = 16384
N_MEM = 256
EPS = 1e-6
NEG = -1e30
POOL_WINDOWS = (2, 4, 8, 16)
POOL_GW = 256
POOL_W = 1024
POOL_HALO = 16
SWA_CONFIGS = ((128, 1), (512, 4), (2048, 16))
SWA_HPG = 4
SWA_HEADS = 12
HEAD_DIM = 128
SWA_OUT = 512
BLK = 128
GLA_HEADS = 4
GLA_DK = 128
GLA_DV = 256
GLA_K = 512
GLA_V = 1024
GLA_RANK = 16
GLA_TAU = 16.0
GLA_CHUNK = 64
MEM_HEADS = 4
MEM_W = 512
N_BRANCH = 4
LANE = 128

OFF_A_VAL, OFF_A_GATE = 0, 1024
OFF_SQ, OFF_SK, OFF_SV, OFF_SGATE = 2048, 3584, 5120, 6656
OFF_CQ, OFF_CK, OFF_CV, OFF_CGATE, OFF_CLR = 7168, 7680, 8192, 9216, 10240
OFF_MQ, OFF_MGATE, OFF_GLOG = 10256, 10768, 11280

MAIN_W = 8192
CB_AVAL, CB_AGATE = 0, 2
CB_Q0, CB_K0, CB_V0, CB_SGATE = 4, 5, 6, 7
CB_CQ, CB_CK, CB_CV, CB_CGATE = 8, 9, 10, 12
CB_MQ, CB_MGATE = 14, 15

MIB = 1024 * 1024


def _params(sem, vmem_mib):
    return pltpu.CompilerParams(dimension_semantics=sem, vmem_limit_bytes=vmem_mib * MIB)


def _sigmoid(x):
    return 1.0 / (1.0 + jnp.exp(-x))


def _silu(x):
    return x * _sigmoid(x)


def _rmsnorm_kernel(x_ref, g_ref, h_ref, *rest, dils, tm):
    dil_refs, hf_ref = rest[:-1], rest[-1]
    x = x_ref[...]
    ms = jnp.mean(x * x, axis=-1, keepdims=True)
    hf = x * lax.rsqrt(ms + EPS) * g_ref[...]
    h_ref[...] = hf.astype(h_ref.dtype)
    nslab = hf.shape[1] // LANE
    for c in range(nslab):
        hf_ref[c] = hf[:, c * LANE:(c + 1) * LANE]
    for ref, dil in zip(dil_refs, dils):
        for r in range(dil):
            rows = pl.ds(r, tm // dil, stride=dil)
            ref[r] = jnp.concatenate([hf_ref[c, rows, :] for c in range(nslab)], axis=1).astype(ref.dtype)


def _rmsnorm(x, g, dils, tm=512):
    s, d = x.shape
    outs = [jax.ShapeDtypeStruct((s, d), BF16)] + [jax.ShapeDtypeStruct((dil, s // dil, d), BF16) for dil in dils]
    out_specs = [pl.BlockSpec((tm, d), lambda i: (i, 0))] + [
        pl.BlockSpec((dil, tm // dil, d), lambda i: (0, i, 0)) for dil in dils]
    return pl.pallas_call(
        functools.partial(_rmsnorm_kernel, dils=dils, tm=tm),
        out_shape=outs,
        grid=(s // tm,),
        in_specs=[pl.BlockSpec((tm, d), lambda i: (i, 0)),
                  pl.BlockSpec((1, d), lambda i: (0, 0))],
        out_specs=out_specs,
        scratch_shapes=[pltpu.VMEM((d // LANE, tm, LANE), F32)],
        compiler_params=_params(("parallel",), 40),
        name="rmsnorm",
    )(x, g.reshape(1, d))


def _matmul_kernel(h_ref, w_ref, o_ref):
    o_ref[...] = jnp.dot(h_ref[...], w_ref[...], preferred_element_type=F32).astype(o_ref.dtype)


def _project(h, w, *, tm=1024, tn=1024, name="project"):
    s, d = h.shape
    n = w.shape[1]
    tn = min(tn, n)
    return pl.pallas_call(
        _matmul_kernel,
        out_shape=jax.ShapeDtypeStruct((s, n), BF16),
        grid=(s // tm, n // tn),
        in_specs=[pl.BlockSpec((tm, d), lambda i, j: (i, 0)),
                  pl.BlockSpec((d, tn), lambda i, j: (0, j))],
        out_specs=pl.BlockSpec((tm, tn), lambda i, j: (i, j)),
        compiler_params=_params(("parallel", "arbitrary"), 48),
        name=name,
    )(h, w)


def _pool_kernel(u_ref, halo_ref, gate_ref, w_ref, sc_ref, o_ref, *, tp):
    i = pl.program_id(0)
    u = u_ref[...].astype(F32)
    halo = jnp.where(i == 0, 0.0, halo_ref[...].astype(F32))
    full = jnp.concatenate([halo, u], axis=0)
    pos = i * tp + lax.broadcasted_iota(jnp.int32, (tp, 1), 0)
    for g, w in enumerate(POOL_WINDOWS):
        cols = slice(g * POOL_GW, (g + 1) * POOL_GW)
        s = full[:, cols]
        k = 1
        while k < w:
            s = s + pltpu.roll(s, k, axis=0)
            k *= 2
        s = s[POOL_HALO:]
        cnt = jnp.minimum(pos + 1, w).astype(F32)
        pooled = s / cnt - u[:, cols]
        mixed = jnp.dot(pooled.astype(BF16), w_ref[g], preferred_element_type=F32) * sc_ref[:, cols]
        gate = gate_ref[:, cols].astype(F32)
        o_ref[:, cols] = (mixed * _silu(gate)).astype(o_ref.dtype)


def _pool_mixer(main, w_pool, pool_scale, tp=512):
    s = main.shape[0]
    hb = tp // POOL_HALO
    return pl.pallas_call(
        functools.partial(_pool_kernel, tp=tp),
        out_shape=jax.ShapeDtypeStruct((s, POOL_W), BF16),
        grid=(s // tp,),
        in_specs=[pl.BlockSpec((tp, POOL_W), lambda i: (i, CB_AVAL // 2)),
                  pl.BlockSpec((POOL_HALO, POOL_W), lambda i: (jnp.maximum(i * hb - 1, 0), CB_AVAL // 2)),
                  pl.BlockSpec((tp, POOL_W), lambda i: (i, CB_AGATE // 2)),
                  pl.BlockSpec((4, POOL_GW, POOL_GW), lambda i: (0, 0, 0)),
                  pl.BlockSpec((1, POOL_W), lambda i: (0, 0))],
        out_specs=pl.BlockSpec((tp, POOL_W), lambda i: (i, 0)),
        compiler_params=_params(("parallel",), 32),
        name="pool_mixer",
    )(main, main, main, w_pool, pool_scale.reshape(1, POOL_W))


def _swa_kernel(q_ref, kc_ref, kp_ref, vc_ref, vp_ref, o_ref, lse_ref, *, dil, slopes, nb):
    n = pl.program_id(0)
    qi = lax.broadcasted_iota(jnp.int32, (BLK, 2 * BLK), 0)
    kj = lax.broadcasted_iota(jnp.int32, (BLK, 2 * BLK), 1)
    delta = qi + BLK - kj
    in_window = (delta >= 0) & (delta <= BLK)
    dist = (delta * dil).astype(F32)
    lane = lax.broadcasted_iota(jnp.int32, (BLK, LANE), 1)
    scale = HEAD_DIM ** -0.5

    def unit(u, carry):
        r = lax.shift_right_logical(u, nb.bit_length() - 1)
        b = u & (nb - 1)
        row0 = pl.multiple_of(b * BLK, BLK)
        prow0 = pl.multiple_of(jnp.maximum(b - 1, 0) * BLK, BLK)
        first = b == 0
        k_prev = jnp.where(first, kp_ref[r], kc_ref[r, pl.ds(prow0, BLK), :])
        v_prev = jnp.where(first, vp_ref[r], vc_ref[r, pl.ds(prow0, BLK), :])
        k_blk = jnp.concatenate([k_prev, kc_ref[r, pl.ds(row0, BLK), :]], axis=0)
        v_blk = jnp.concatenate([v_prev, vc_ref[r, pl.ds(row0, BLK), :]], axis=0)
        q_blk = q_ref[r, pl.ds(row0, BLK), :]
        valid = in_window & jnp.logical_not(((n == 0) & first) & (kj < BLK))
        rows = pl.ds(row0, BLK) if dil == 1 else pl.ds(b * (BLK * dil) + r, BLK, stride=dil)
        lse_acc = jnp.zeros((BLK, LANE), F32)
        for h in range(SWA_HPG):
            hc = slice(h * HEAD_DIM, (h + 1) * HEAD_DIM)
            s = lax.dot_general(q_blk[:, hc], k_blk[:, hc], (((1,), (1,)), ((), ())),
                                preferred_element_type=F32) * scale
            s = jnp.where(valid, s - slopes[h] * dist, NEG)
            m = jnp.max(s, axis=-1, keepdims=True)
            e = jnp.exp(s - m)
            den = jnp.sum(e, axis=-1, keepdims=True)
            p = (e / den).astype(BF16)
            o_ref[h, rows, :] = jnp.dot(p, v_blk[:, hc], preferred_element_type=F32)
            lse_acc = jnp.where(lane == h, m + jnp.log(den), lse_acc)
        lse_ref[rows, :] = lse_acc
        return carry

    lax.fori_loop(0, dil * nb, unit, 0, unroll=SWA_UNROLL)


SWA_STEP = 2048
SWA_UNROLL = 4


def _swa_group(qkv, cols, group):
    window, dil = SWA_CONFIGS[group]
    sub = qkv.shape[1]
    s = sub * dil
    tq = SWA_STEP // dil
    nb = tq // BLK
    qc, kc, vc = cols
    slopes = tuple(float(2.0 ** (-8.0 * (group * SWA_HPG + h + 1.0) / SWA_HEADS)) for h in range(SWA_HPG))
    cur = lambda c: pl.BlockSpec((dil, tq, SWA_OUT), lambda n: (0, n, c))
    prev = lambda c: pl.BlockSpec((dil, BLK, SWA_OUT), lambda n: (0, jnp.maximum(n * nb - 1, 0), c))
    return pl.pallas_call(
        functools.partial(_swa_kernel, dil=dil, slopes=slopes, nb=nb),
        out_shape=(jax.ShapeDtypeStruct((SWA_HPG, s, HEAD_DIM), F32),
                   jax.ShapeDtypeStruct((s, LANE), F32)),
        grid=(s // SWA_STEP,),
        in_specs=[cur(qc), cur(kc), prev(kc), cur(vc), prev(vc)],
        out_specs=(pl.BlockSpec((SWA_HPG, SWA_STEP, HEAD_DIM), lambda n: (0, n, 0)),
                   pl.BlockSpec((SWA_STEP, LANE), lambda n: (n, 0))),
        compiler_params=_params(("parallel",), 48),
        name=f"swa_group{group}",
    )(qkv, qkv, qkv, qkv, qkv)


def _swa_combine_kernel(o0, o1, o2, l0, l1, l2, gate_ref, y_ref):
    ls = [l0[...], l1[...], l2[...]]
    mx = jnp.maximum(jnp.maximum(ls[0], ls[1]), ls[2])
    es = [jnp.exp(l - mx) for l in ls]
    den = es[0] + es[1] + es[2]
    ws = [e / den for e in es]
    os_ = [o0, o1, o2]
    for h in range(SWA_HPG):
        hc = slice(h * HEAD_DIM, (h + 1) * HEAD_DIM)
        acc = None
        for g in range(3):
            term = ws[g][:, h:h + 1] * os_[g][h]
            acc = term if acc is None else acc + term
        gate = gate_ref[:, hc].astype(F32)
        y_ref[:, hc] = (acc * _silu(gate)).astype(y_ref.dtype)


def _swa_combine(os_, lses, main, tm=512):
    s = main.shape[0]
    ospec = pl.BlockSpec((SWA_HPG, tm, HEAD_DIM), lambda i: (0, i, 0))
    lspec = pl.BlockSpec((tm, LANE), lambda i: (i, 0))
    return pl.pallas_call(
        _swa_combine_kernel,
        out_shape=jax.ShapeDtypeStruct((s, SWA_OUT), BF16),
        grid=(s // tm,),
        in_specs=[ospec, ospec, ospec, lspec, lspec, lspec,
                  pl.BlockSpec((tm, SWA_OUT), lambda i: (i, CB_SGATE))],
        out_specs=pl.BlockSpec((tm, SWA_OUT), lambda i: (i, 0)),
        compiler_params=_params(("parallel",), 32),
        name="swa_combine",
    )(*os_, *lses, main)


def _gla_kernel(q_ref, k_ref, v_ref, gate_ref, lr_ref, wa_ref, ba_ref, gg_ref, o_ref, state_ref, *, tt):
    @pl.when(pl.program_id(0) == 0)
    def _():
        state_ref[...] = jnp.zeros_like(state_ref)

    c = GLA_CHUNK
    z = jnp.dot(lr_ref[...], wa_ref[...], preferred_element_type=F32) + ba_ref[...]
    log_a = (jnp.minimum(z, 0.0) - jnp.log1p(jnp.exp(-jnp.abs(z)))) / GLA_TAU
    in_chunk = lax.broadcasted_iota(jnp.int32, (tt, GLA_K), 0) & (c - 1)
    bc = log_a
    k = 1
    while k < c:
        bc = bc + jnp.where(in_chunk >= k, pltpu.roll(bc, k, axis=0), 0.0)
        k *= 2
    q_t = (q_ref[...].astype(F32) * (GLA_DK ** -0.5)) * jnp.exp(bc)
    kf = k_ref[...].astype(F32)
    k_t = kf * jnp.exp(-bc)
    q_tb = q_t.astype(BF16)
    k_tb = k_t.astype(BF16)
    ri = lax.broadcasted_iota(jnp.int32, (tt, tt), 0)
    ci = lax.broadcasted_iota(jnp.int32, (tt, tt), 1)
    causal = ((ri & -c) == (ci & -c)) & (ci <= ri)
    nchunk = tt // c
    b_last = [bc[(j + 1) * c - 1:(j + 1) * c, :] for j in range(nchunk)]
    k_dec = jnp.concatenate(
        [kf[j * c:(j + 1) * c, :] * jnp.exp(b_last[j] - bc[j * c:(j + 1) * c, :]) for j in range(nchunk)],
        axis=0).astype(BF16)
    decay = [jnp.exp(b) for b in b_last]
    for h in range(GLA_HEADS):
        kc = slice(h * GLA_DK, (h + 1) * GLA_DK)
        vc = slice(h * GLA_DV, (h + 1) * GLA_DV)
        v_h = v_ref[:, vc]
        a = lax.dot_general(q_tb[:, kc], k_tb[:, kc], (((1,), (1,)), ((), ())), preferred_element_type=F32)
        a = jnp.where(causal, a, 0.0)
        o = jnp.dot(a.astype(BF16), v_h, preferred_element_type=F32)
        state = state_ref[h]
        inter = []
        for j in range(nchunk):
            rows = slice(j * c, (j + 1) * c)
            inter.append(lax.dot_general(q_tb[rows, kc], state.astype(BF16), (((1,), (1,)), ((), ())),
                                         preferred_element_type=F32))
            kv_t = lax.dot_general(v_h[rows, :], k_dec[rows, kc], (((0,), (0,)), ((), ())),
                                   preferred_element_type=F32)
            state = decay[j][:, kc] * state + kv_t
        state_ref[h] = state
        o = o + jnp.concatenate(inter, axis=0)
        o = o * lax.rsqrt(jnp.mean(o * o, axis=-1, keepdims=True) + EPS)
        o = o * gg_ref[:, vc]
        gate = gate_ref[:, vc].astype(F32)
        o_ref[:, vc] = (o * _silu(gate)).astype(o_ref.dtype)


def _gla_mixer(main, lr, w_alpha_pad, b_alpha, g_gla, tt=256):
    s = main.shape[0]
    return pl.pallas_call(
        functools.partial(_gla_kernel, tt=tt),
        out_shape=jax.ShapeDtypeStruct((s, GLA_V), BF16),
        grid=(s // tt,),
        in_specs=[pl.BlockSpec((tt, GLA_K), lambda i: (i, CB_CQ)),
                  pl.BlockSpec((tt, GLA_K), lambda i: (i, CB_CK)),
                  pl.BlockSpec((tt, GLA_V), lambda i: (i, CB_CV // 2)),
                  pl.BlockSpec((tt, GLA_V), lambda i: (i, CB_CGATE // 2)),
                  pl.BlockSpec((tt, LANE), lambda i: (i, 0)),
                  pl.BlockSpec((LANE, GLA_K), lambda i: (0, 0)),
                  pl.BlockSpec((1, GLA_K), lambda i: (0, 0)),
                  pl.BlockSpec((1, GLA_V), lambda i: (0, 0))],
        out_specs=pl.BlockSpec((tt, GLA_V), lambda i: (i, 0)),
        scratch_shapes=[pltpu.VMEM((GLA_HEADS, GLA_DV, GLA_DK), F32)],
        compiler_params=_params(("arbitrary",), 32),
        name="gla_mixer",
    )(main, main, main, main, lr, w_alpha_pad, b_alpha.reshape(1, GLA_K), g_gla.reshape(1, GLA_V))


def _mem_kv_kernel(mem_ref, g_ref, w_ref, o_ref):
    x = mem_ref[...]
    ms = jnp.mean(x * x, axis=-1, keepdims=True)
    hm = (x * lax.rsqrt(ms + EPS) * g_ref[...]).astype(BF16)
    o_ref[...] = jnp.dot(hm, w_ref[...], preferred_element_type=F32).astype(o_ref.dtype)


def _mem_kv(mem, g_mem, w_mem_kv):
    return pl.pallas_call(
        _mem_kv_kernel,
        out_shape=jax.ShapeDtypeStruct((N_MEM, 2 * MEM_W), BF16),
        compiler_params=pltpu.CompilerParams(vmem_limit_bytes=32 * MIB),
        name="mem_kv",
    )(mem, g_mem.reshape(1, D_MODEL), w_mem_kv)


def _mem_attn_kernel(q_ref, gate_ref, kv_ref, o_ref):
    scale = HEAD_DIM ** -0.5
    for h in range(MEM_HEADS):
        hc = slice(h * HEAD_DIM, (h + 1) * HEAD_DIM)
        km = kv_ref[:, h * HEAD_DIM:(h + 1) * HEAD_DIM]
        vm = kv_ref[:, MEM_W + h * HEAD_DIM:MEM_W + (h + 1) * HEAD_DIM]
        s = lax.dot_general(q_ref[:, hc], km, (((1,), (1,)), ((), ())), preferred_element_type=F32) * scale
        m = jnp.max(s, axis=-1, keepdims=True)
        e = jnp.exp(s - m)
        p = (e / jnp.sum(e, axis=-1, keepdims=True)).astype(BF16)
        o = jnp.dot(p, vm, preferred_element_type=F32)
        gate = gate_ref[:, hc].astype(F32)
        o_ref[:, hc] = (o * _silu(gate)).astype(o_ref.dtype)


def _mem_attn(main, kv, tm=512):
    s = main.shape[0]
    return pl.pallas_call(
        _mem_attn_kernel,
        out_shape=jax.ShapeDtypeStruct((s, MEM_W), BF16),
        grid=(s // tm,),
        in_specs=[pl.BlockSpec((tm, MEM_W), lambda i: (i, CB_MQ)),
                  pl.BlockSpec((tm, MEM_W), lambda i: (i, CB_MGATE)),
                  pl.BlockSpec((N_MEM, 2 * MEM_W), lambda i: (0, 0))],
        out_specs=pl.BlockSpec((tm, MEM_W), lambda i: (i, 0)),
        compiler_params=_params(("parallel",), 32),
        name="mem_attn",
    )(main, main, kv)


def _merge_kernel(ya_ref, yb_ref, yc_ref, ym_ref, gl_ref, bm_ref, wa_ref, wb_ref, wc_ref, wm_ref, wo_ref,
                  gp_ref, x_ref, o_ref):
    merged = None
    for j, (y_ref, w_ref) in enumerate(((ya_ref, wa_ref), (yb_ref, wb_ref), (yc_ref, wc_ref), (ym_ref, wm_ref))):
        br = jnp.dot(y_ref[...], w_ref[...], preferred_element_type=F32)
        gate = _sigmoid(gl_ref[:, j * D_MODEL:(j + 1) * D_MODEL].astype(F32) + bm_ref[j:j + 1, :])
        term = gate * br
        merged = term if merged is None else merged + term
    out = jnp.dot(merged.astype(BF16), wo_ref[...], preferred_element_type=F32)
    ms = jnp.mean(out * out, axis=-1, keepdims=True)
    o_ref[...] = x_ref[...] + out * lax.rsqrt(ms + EPS) * gp_ref[...]


def _merge(ya, yb, yc, ym, glog, b_merge, wa, wb, wc, wm, wo, g_post, x, tm=256):
    s = x.shape[0]
    row = lambda w: pl.BlockSpec((tm, w), lambda i: (i, 0))
    const = lambda a: pl.BlockSpec(a.shape, lambda i: (0,) * a.ndim, pipeline_mode=pl.Buffered(1))
    g_post = g_post.reshape(1, D_MODEL)
    return pl.pallas_call(
        _merge_kernel,
        out_shape=jax.ShapeDtypeStruct((s, D_MODEL), F32),
        grid=(s // tm,),
        in_specs=[row(POOL_W), row(SWA_OUT), row(GLA_V), row(MEM_W), row(N_BRANCH * D_MODEL),
                  const(b_merge), const(wa), const(wb), const(wc), const(wm), const(wo), const(g_post),
                  row(D_MODEL)],
        out_specs=row(D_MODEL),
        compiler_params=_params(("parallel",), 56),
        name="merge_out",
    )(ya, yb, yc, ym, glog, b_merge, wa, wb, wc, wm, wo, g_post, x)


def _layer(x, mem, g_pre, g_post, g_mem, w_in, b_merge, w_pool, pool_scale, w_alpha, b_alpha, g_gla,
           w_mem_kv, w_br_pool, w_br_swa, w_br_gla, w_br_mem, w_out):
    def cols(off, width):
        return w_in[:, off:off + width]

    def swa_cols(off, group):
        return cols(off + group * SWA_OUT, SWA_OUT)

    w_main = jnp.concatenate([
        cols(OFF_A_VAL, 2 * POOL_W),
        swa_cols(OFF_SQ, 0), swa_cols(OFF_SK, 0), swa_cols(OFF_SV, 0), cols(OFF_SGATE, SWA_OUT),
        cols(OFF_CQ, 2 * GLA_K + 2 * GLA_V),
        cols(OFF_MQ, 2 * MEM_W)], axis=1).astype(BF16)
    w_lr = jnp.pad(cols(OFF_CLR, GLA_RANK), ((0, 0), (0, LANE - GLA_RANK))).astype(BF16)
    w_glog = cols(OFF_GLOG, N_BRANCH * D_MODEL).astype(BF16)
    w_dil = [jnp.concatenate([swa_cols(OFF_SQ, g), swa_cols(OFF_SK, g), swa_cols(OFF_SV, g)], axis=1).astype(BF16)
             for g in (1, 2)]
    w_alpha_pad = jnp.pad(w_alpha, ((0, LANE - GLA_RANK), (0, 0))).astype(BF16)

    s = x.shape[0]
    dils = tuple(dil for _, dil in SWA_CONFIGS[1:])
    h, h_d1, h_d2 = _rmsnorm(x, g_pre, dils)
    main = _project(h, w_main, name="project_main")
    lr = _project(h, w_lr, name="project_lr")
    glog = _project(h, w_glog, name="project_gates")
    qkv1 = _project(h_d1.reshape(s, D_MODEL), w_dil[0], tn=512, name="project_dil4")
    qkv2 = _project(h_d2.reshape(s, D_MODEL), w_dil[1], tn=512, name="project_dil16")

    ya = _pool_mixer(main, w_pool.astype(BF16), pool_scale)
    o0, l0 = _swa_group(main.reshape(1, s, MAIN_W), (CB_Q0, CB_K0, CB_V0), 0)
    o1, l1 = _swa_group(qkv1.reshape(dils[0], s // dils[0], 3 * SWA_OUT), (0, 1, 2), 1)
    o2, l2 = _swa_group(qkv2.reshape(dils[1], s // dils[1], 3 * SWA_OUT), (0, 1, 2), 2)
    yb = _swa_combine((o0, o1, o2), (l0, l1, l2), main)
    yc = _gla_mixer(main, lr, w_alpha_pad, b_alpha, g_gla)
    kv = _mem_kv(mem, g_mem, w_mem_kv.astype(BF16))
    ym = _mem_attn(main, kv)
    return _merge(ya, yb, yc, ym, glog, b_merge, w_br_pool.astype(BF16), w_br_swa.astype(BF16),
                  w_br_gla.astype(BF16), w_br_mem.astype(BF16), w_out.astype(BF16), g_post, x)


def kernel(x, mem, g_pre, g_post, g_mem, w_in, b_merge, w_pool, pool_scale, w_alpha, b_alpha, g_gla, w_mem_kv,
           w_br_pool, w_br_swa, w_br_gla, w_br_mem, w_out):
    b, s, d = x.shape
    assert (b, s, d) == (1, SEQ, D_MODEL) and mem.shape == (1, N_MEM, D_MODEL)
    xs = x[0]
    for l in range(g_pre.shape[0]):
        xs = _layer(xs, mem[0], g_pre[l], g_post[l], g_mem[l], w_in[l], b_merge[l], w_pool[l], pool_scale[l],
                    w_alpha[l], b_alpha[l], g_gla[l], w_mem_kv[l], w_br_pool[l], w_br_swa[l], w_br_gla[l],
                    w_br_mem[l], w_out[l])
    return xs[None]
```

```python
import functools

import jax
import jax.numpy as jnp
from jax import lax
from jax.experimental import pallas as pl
from jax.experimental.pallas import tpu as pltpu

F32 = jnp.float32
BF16 = jnp.bfloat16

D_MODEL = 2048
SEQ = 16384
N_MEM = 256
EPS = 1e-6
NEG = -1e30
POOL_WINDOWS = (2, 4, 8, 16)
POOL_GW = 256
POOL_W = 1024
POOL_HALO = 16
SWA_CONFIGS = ((128, 1), (512, 4), (2048, 16))
SWA_HPG = 4
SWA_HEADS = 12
HEAD_DIM = 128
SWA_OUT = 512
BLK = 128
GLA_HEADS = 4
GLA_DK = 128
GLA_DV = 256
GLA_K = 512
GLA_V = 1024
GLA_RANK = 16
GLA_TAU = 16.0
GLA_CHUNK = 64
MEM_HEADS = 4
MEM_W = 512
N_BRANCH = 4
LANE = 128

OFF_A_VAL, OFF_A_GATE = 0, 1024
OFF_SQ, OFF_SK, OFF_SV, OFF_SGATE = 2048, 3584, 5120, 6656
OFF_CQ, OFF_CK, OFF_CV, OFF_CGATE, OFF_CLR = 7168, 7680, 8192, 9216, 10240
OFF_MQ, OFF_MGATE, OFF_GLOG = 10256, 10768, 11280

PROJ_TN = 512
SWA_STEP = 2048
SWA_UNROLL = 4


def _tiles(off, width, tn=PROJ_TN, shift=0):
    assert (off - shift) % tn == 0 and width % tn == 0
    return tuple(range((off - shift) // tn, (off - shift + width) // tn))


def _swa_tiles(group):
    return tuple(t for off in (OFF_SQ, OFF_SK, OFF_SV) for t in _tiles(off + group * SWA_OUT, SWA_OUT))


MAIN_TILES = (_tiles(OFF_A_VAL, 2 * POOL_W) + _swa_tiles(0) + _tiles(OFF_SGATE, SWA_OUT)
              + _tiles(OFF_CQ, 2 * GLA_K + 2 * GLA_V))
MAIN_W = len(MAIN_TILES) * PROJ_TN
CB_AVAL, CB_AGATE = 0, 2
CB_Q0, CB_K0, CB_V0, CB_SGATE = 4, 5, 6, 7
CB_CQ, CB_CK, CB_CV, CB_CGATE = 8, 9, 10, 12
TAIL_SHIFT = OFF_MQ % LANE
TAIL_TILES = (_tiles(OFF_GLOG, N_BRANCH * D_MODEL, shift=TAIL_SHIFT) + _tiles(OFF_MQ, 2 * MEM_W, shift=TAIL_SHIFT))
CB_GLOG, CB_MQ, CB_MGATE = 0, 16, 17
LR_TILES = _tiles(OFF_CLR, LANE, tn=LANE)

MIB = 1024 * 1024


def _params(sem, vmem_mib):
    return pltpu.CompilerParams(dimension_semantics=sem, vmem_limit_bytes=vmem_mib * MIB)


def _sigmoid(x):
    return 1.0 / (1.0 + jnp.exp(-x))


def _silu(x):
    return x * _sigmoid(x)


def _rmsnorm_kernel(x_ref, g_ref, h_ref, *rest, dils, tm):
    nd = len(dils)
    dil_refs, slab_refs = rest[:nd], rest[nd:]
    x = x_ref[...]
    ms = jnp.mean(x * x, axis=-1, keepdims=True)
    hf = x * lax.rsqrt(ms + EPS) * g_ref[...]
    h_ref[...] = hf.astype(h_ref.dtype)
    nslab = hf.shape[1] // LANE
    for c in range(nslab):
        slab_refs[0][c] = hf[:, c * LANE:(c + 1) * LANE]
    prev = 1
    for k, (ref, dil) in enumerate(zip(dil_refs, dils)):
        ratio, n = dil // prev, tm // dil
        for g in range(prev):
            for q in range(ratio):
                r = g + prev * q
                rows = pl.ds(g * (tm // prev) + q, n, stride=ratio)
                pieces = [slab_refs[k][c, rows, :] for c in range(nslab)]
                ref[r] = jnp.concatenate(pieces, axis=1).astype(ref.dtype)
                if k + 1 < nd:
                    for c in range(nslab):
                        slab_refs[k + 1][c, r * n:(r + 1) * n, :] = pieces[c]
        prev = dil


def _rmsnorm(x, g, dils, tm=512):
    s, d = x.shape
    assert all(b % a == 0 for a, b in zip((1,) + dils, dils))
    outs = [jax.ShapeDtypeStruct((s, d), BF16)] + [jax.ShapeDtypeStruct((dil, s // dil, d), BF16) for dil in dils]
    out_specs = [pl.BlockSpec((tm, d), lambda i: (i, 0))] + [
        pl.BlockSpec((dil, tm // dil, d), lambda i: (0, i, 0)) for dil in dils]
    return pl.pallas_call(
        functools.partial(_rmsnorm_kernel, dils=dils, tm=tm),
        out_shape=outs,
        grid=(s // tm,),
        in_specs=[pl.BlockSpec((tm, d), lambda i: (i, 0)),
                  pl.BlockSpec((1, d), lambda i: (0, 0))],
        out_specs=out_specs,
        scratch_shapes=[pltpu.VMEM((d // LANE, tm, LANE), F32) for _ in dils],
        compiler_params=_params(("parallel",), 40),
        name="rmsnorm",
    )(x, g.reshape(1, d))


def _project_kernel(tiles_ref, h_ref, w_ref, *rest, shift, tn):
    del tiles_ref
    if shift:
        wx_ref, o_ref, wb_ref = rest
    else:
        o_ref, wb_ref = rest

    @pl.when(pl.program_id(1) == 0)
    def _():
        if shift:
            w = jnp.concatenate([w_ref[...], wx_ref[...]], axis=1)
            wb_ref[...] = w[:, shift:shift + tn].astype(wb_ref.dtype)
        else:
            wb_ref[...] = w_ref[...].astype(wb_ref.dtype)

    o_ref[...] = jnp.dot(h_ref[...], wb_ref[...], preferred_element_type=F32).astype(o_ref.dtype)


def _project(h, w_in, layer, tiles, *, tn, shift=0, tm=2048, name="project"):
    s, d = h.shape
    nj, ni = len(tiles), s // tm
    in_specs = [pl.BlockSpec((tm, d), lambda j, i, t: (i, 0)),
                pl.BlockSpec((None, d, tn), lambda j, i, t: (layer, 0, t[j]))]
    operands = [h, w_in]
    if shift:
        in_specs.append(pl.BlockSpec((None, d, LANE), lambda j, i, t: (layer, 0, (t[j] + 1) * (tn // LANE))))
        operands.append(w_in)
    return pl.pallas_call(
        functools.partial(_project_kernel, shift=shift, tn=tn),
        out_shape=jax.ShapeDtypeStruct((s, nj * tn), BF16),
        grid_spec=pltpu.PrefetchScalarGridSpec(
            num_scalar_prefetch=1,
            grid=(nj, ni),
            in_specs=in_specs,
            out_specs=pl.BlockSpec((tm, tn), lambda j, i, t: (i, j)),
            scratch_shapes=[pltpu.VMEM((d, tn), BF16)]),
        compiler_params=_params(("arbitrary", "arbitrary"), 48),
        name=name,
    )(jnp.asarray(tiles, jnp.int32), *operands)


def _pool_kernel(u_ref, halo_ref, gate_ref, w_ref, sc_ref, o_ref, *, tp):
    i = pl.program_id(0)
    u = u_ref[...].astype(F32)
    halo = jnp.where(i == 0, 0.0, halo_ref[...].astype(F32))
    full = jnp.concatenate([halo, u], axis=0)
    pos = i * tp + lax.broadcasted_iota(jnp.int32, (tp, 1), 0)
    for g, w in enumerate(POOL_WINDOWS):
        cols = slice(g * POOL_GW, (g + 1) * POOL_GW)
        s = full[:, cols]
        k = 1
        while k < w:
            s = s + pltpu.roll(s, k, axis=0)
            k *= 2
        s = s[POOL_HALO:]
        inv_cnt = 1.0 / jnp.minimum(pos + 1, w).astype(F32)
        pooled = s * inv_cnt - u[:, cols]
        mixed = jnp.dot(pooled.astype(BF16), w_ref[g], preferred_element_type=F32) * sc_ref[:, cols]
        gate = gate_ref[:, cols].astype(F32)
        o_ref[:, cols] = (mixed * _silu(gate)).astype(o_ref.dtype)


def _pool_mixer(main, w_pool, pool_scale, tp=512):
    s = main.shape[0]
    hb = tp // POOL_HALO
    return pl.pallas_call(
        functools.partial(_pool_kernel, tp=tp),
        out_shape=jax.ShapeDtypeStruct((s, POOL_W), BF16),
        grid=(s // tp,),
        in_specs=[pl.BlockSpec((tp, POOL_W), lambda i: (i, CB_AVAL // 2)),
                  pl.BlockSpec((POOL_HALO, POOL_W), lambda i: (jnp.maximum(i * hb - 1, 0), CB_AVAL // 2)),
                  pl.BlockSpec((tp, POOL_W), lambda i: (i, CB_AGATE // 2)),
                  pl.BlockSpec((4, POOL_GW, POOL_GW), lambda i: (0, 0, 0)),
                  pl.BlockSpec((1, POOL_W), lambda i: (0, 0))],
        out_specs=pl.BlockSpec((tp, POOL_W), lambda i: (i, 0)),
        compiler_params=_params(("parallel",), 32),
        name="pool_mixer",
    )(main, main, main, w_pool, pool_scale.reshape(1, POOL_W))


def _swa_kernel(q_ref, kc_ref, kp_ref, vc_ref, vp_ref, o_ref, lse_ref, *, dil, slopes, nb):
    n = pl.program_id(0)
    qi = lax.broadcasted_iota(jnp.int32, (BLK, 2 * BLK), 0)
    kj = lax.broadcasted_iota(jnp.int32, (BLK, 2 * BLK), 1)
    delta = qi + BLK - kj
    in_window = (delta >= 0) & (delta <= BLK)
    dist = (delta * dil).astype(F32)
    lane = lax.broadcasted_iota(jnp.int32, (BLK, LANE), 1)
    scale = HEAD_DIM ** -0.5

    def unit(u, carry):
        r = lax.shift_right_logical(u, nb.bit_length() - 1)
        b = u & (nb - 1)
        row0 = pl.multiple_of(b * BLK, BLK)
        prow0 = pl.multiple_of(jnp.maximum(b - 1, 0) * BLK, BLK)
        first = b == 0
        k_prev = jnp.where(first, kp_ref[r], kc_ref[r, pl.ds(prow0, BLK), :])
        v_prev = jnp.where(first, vp_ref[r], vc_ref[r, pl.ds(prow0, BLK), :])
        k_blk = jnp.concatenate([k_prev, kc_ref[r, pl.ds(row0, BLK), :]], axis=0)
        v_blk = jnp.concatenate([v_prev, vc_ref[r, pl.ds(row0, BLK), :]], axis=0)
        q_blk = q_ref[r, pl.ds(row0, BLK), :]
        valid = in_window & jnp.logical_not(((n == 0) & first) & (kj < BLK))
        rows = pl.ds(row0, BLK) if dil == 1 else pl.ds(b * (BLK * dil) + r, BLK, stride=dil)
        lse_acc = jnp.zeros((BLK, LANE), F32)
        for h in range(SWA_HPG):
            hc = slice(h * HEAD_DIM, (h + 1) * HEAD_DIM)
            s = lax.dot_general(q_blk[:, hc], k_blk[:, hc], (((1,), (1,)), ((), ())),
                                preferred_element_type=F32) * scale
            s = jnp.where(valid, s - slopes[h] * dist, NEG)
            m = jnp.max(s, axis=-1, keepdims=True)
            e = jnp.exp(s - m)
            den = jnp.sum(e, axis=-1, keepdims=True)
            p = (e / den).astype(BF16)
            o_ref[h, rows, :] = jnp.dot(p, v_blk[:, hc], preferred_element_type=F32)
            lse_acc = jnp.where(lane == h, m + jnp.log(den), lse_acc)
        lse_ref[rows, :] = lse_acc
        return carry

    lax.fori_loop(0, dil * nb, unit, 0, unroll=SWA_UNROLL)


def _swa_group(qkv, cols, group):
    window, dil = SWA_CONFIGS[group]
    sub = qkv.shape[1]
    s = sub * dil
    tq = SWA_STEP // dil
    nb = tq // BLK
    qc, kc, vc = cols
    slopes = tuple(float(2.0 ** (-8.0 * (group * SWA_HPG + h + 1.0) / SWA_HEADS)) for h in range(SWA_HPG))
    cur = lambda c: pl.BlockSpec((dil, tq, SWA_OUT), lambda n: (0, n, c))
    prev = lambda c: pl.BlockSpec((dil, BLK, SWA_OUT), lambda n: (0, jnp.maximum(n * nb - 1, 0), c))
    return pl.pallas_call(
        functools.partial(_swa_kernel, dil=dil, slopes=slopes, nb=nb),
        out_shape=(jax.ShapeDtypeStruct((SWA_HPG, s, HEAD_DIM), F32),
                   jax.ShapeDtypeStruct((s, LANE), F32)),
        grid=(s // SWA_STEP,),
        in_specs=[cur(qc), cur(kc), prev(kc), cur(vc), prev(vc)],
        out_specs=(pl.BlockSpec((SWA_HPG, SWA_STEP, HEAD_DIM), lambda n: (0, n, 0)),
                   pl.BlockSpec((SWA_STEP, LANE), lambda n: (n, 0))),
        compiler_params=_params(("parallel",), 48),
        name=f"swa_group{group}",
    )(qkv, qkv, qkv, qkv, qkv)


def _swa_combine_kernel(o0, o1, o2, l0, l1, l2, gate_ref, y_ref):
    ls = [l0[...], l1[...], l2[...]]
    mx = jnp.maximum(jnp.maximum(ls[0], ls[1]), ls[2])
    es = [jnp.exp(l - mx) for l in ls]
    den = es[0] + es[1] + es[2]
    ws = [e / den for e in es]
    os_ = [o0, o1, o2]
    for h in range(SWA_HPG):
        hc = slice(h * HEAD_DIM, (h + 1) * HEAD_DIM)
        acc = None
        for g in range(3):
            term = ws[g][:, h:h + 1] * os_[g][h]
            acc = term if acc is None else acc + term
        gate = gate_ref[:, hc].astype(F32)
        y_ref[:, hc] = (acc * _silu(gate)).astype(y_ref.dtype)


def _swa_combine(os_, lses, main, tm=512):
    s = main.shape[0]
    ospec = pl.BlockSpec((SWA_HPG, tm, HEAD_DIM), lambda i: (0, i, 0))
    lspec = pl.BlockSpec((tm, LANE), lambda i: (i, 0))
    return pl.pallas_call(
        _swa_combine_kernel,
        out_shape=jax.ShapeDtypeStruct((s, SWA_OUT), BF16),
        grid=(s // tm,),
        in_specs=[ospec, ospec, ospec, lspec, lspec, lspec,
                  pl.BlockSpec((tm, SWA_OUT), lambda i: (i, CB_SGATE))],
        out_specs=pl.BlockSpec((tm, SWA_OUT), lambda i: (i, 0)),
        compiler_params=_params(("parallel",), 32),
        name="swa_combine",
    )(*os_, *lses, main)


def _gla_kernel(q_ref, k_ref, v_ref, gate_ref, lr_ref, wa_ref, ba_ref, gg_ref, o_ref, state_ref, *, tt):
    @pl.when(pl.program_id(0) == 0)
    def _():
        state_ref[...] = jnp.zeros_like(state_ref)

    c = GLA_CHUNK
    z = jnp.dot(lr_ref[...], wa_ref[...], preferred_element_type=F32) + ba_ref[...]
    log_a = (jnp.minimum(z, 0.0) - jnp.log1p(jnp.exp(-jnp.abs(z)))) / GLA_TAU
    in_chunk = lax.broadcasted_iota(jnp.int32, (tt, GLA_K), 0) & (c - 1)
    bc = log_a
    k = 1
    while k < c:
        bc = bc + jnp.where(in_chunk >= k, pltpu.roll(bc, k, axis=0), 0.0)
        k *= 2
    q_t = (q_ref[...].astype(F32) * (GLA_DK ** -0.5)) * jnp.exp(bc)
    kf = k_ref[...].astype(F32)
    k_t = kf * jnp.exp(-bc)
    q_tb = q_t.astype(BF16)
    k_tb = k_t.astype(BF16)
    ri = lax.broadcasted_iota(jnp.int32, (tt, tt), 0)
    ci = lax.broadcasted_iota(jnp.int32, (tt, tt), 1)
    causal = ((ri & -c) == (ci & -c)) & (ci <= ri)
    nchunk = tt // c
    b_last = [bc[(j + 1) * c - 1:(j + 1) * c, :] for j in range(nchunk)]
    k_dec = jnp.concatenate(
        [kf[j * c:(j + 1) * c, :] * jnp.exp(b_last[j] - bc[j * c:(j + 1) * c, :]) for j in range(nchunk)],
        axis=0).astype(BF16)
    decay = [jnp.exp(b) for b in b_last]
    for h in range(GLA_HEADS):
        kc = slice(h * GLA_DK, (h + 1) * GLA_DK)
        vc = slice(h * GLA_DV, (h + 1) * GLA_DV)
        v_h = v_ref[:, vc]
        a = lax.dot_general(q_tb[:, kc], k_tb[:, kc], (((1,), (1,)), ((), ())), preferred_element_type=F32)
        a = jnp.where(causal, a, 0.0)
        o = jnp.dot(a.astype(BF16), v_h, preferred_element_type=F32)
        state = state_ref[h]
        inter = []
        for j in range(nchunk):
            rows = slice(j * c, (j + 1) * c)
            inter.append(lax.dot_general(q_tb[rows, kc], state.astype(BF16), (((1,), (1,)), ((), ())),
                                         preferred_element_type=F32))
            kv_t = lax.dot_general(v_h[rows, :], k_dec[rows, kc], (((0,), (0,)), ((), ())),
                                   preferred_element_type=F32)
            state = decay[j][:, kc] * state + kv_t
        state_ref[h] = state
        o = o + jnp.concatenate(inter, axis=0)
        o = o * lax.rsqrt(jnp.mean(o * o, axis=-1, keepdims=True) + EPS)
        o = o * gg_ref[:, vc]
        gate = gate_ref[:, vc].astype(F32)
        o_ref[:, vc] = (o * _silu(gate)).astype(o_ref.dtype)


def _gla_mixer(main, lr, w_alpha_pad, b_alpha, g_gla, tt=256):
    s = main.shape[0]
    return pl.pallas_call(
        functools.partial(_gla_kernel, tt=tt),
        out_shape=jax.ShapeDtypeStruct((s, GLA_V), BF16),
        grid=(s // tt,),
        in_specs=[pl.BlockSpec((tt, GLA_K), lambda i: (i, CB_CQ)),
                  pl.BlockSpec((tt, GLA_K), lambda i: (i, CB_CK)),
                  pl.BlockSpec((tt, GLA_V), lambda i: (i, CB_CV // 2)),
                  pl.BlockSpec((tt, GLA_V), lambda i: (i, CB_CGATE // 2)),
                  pl.BlockSpec((tt, LANE), lambda i: (i, 0)),
                  pl.BlockSpec((LANE, GLA_K), lambda i: (0, 0)),
                  pl.BlockSpec((1, GLA_K), lambda i: (0, 0)),
                  pl.BlockSpec((1, GLA_V), lambda i: (0, 0))],
        out_specs=pl.BlockSpec((tt, GLA_V), lambda i: (i, 0)),
        scratch_shapes=[pltpu.VMEM((GLA_HEADS, GLA_DV, GLA_DK), F32)],
        compiler_params=_params(("arbitrary",), 32),
        name="gla_mixer",
    )(main, main, main, main, lr, w_alpha_pad, b_alpha.reshape(1, GLA_K), g_gla.reshape(1, GLA_V))


def _mem_kv_kernel(mem_ref, g_ref, w_ref, o_ref):
    x = mem_ref[...]
    ms = jnp.mean(x * x, axis=-1, keepdims=True)
    hm = (x * lax.rsqrt(ms + EPS) * g_ref[...]).astype(BF16)
    o_ref[...] = jnp.dot(hm, w_ref[...], preferred_element_type=F32).astype(o_ref.dtype)


def _mem_kv(mem, g_mem, w_mem_kv):
    return pl.pallas_call(
        _mem_kv_kernel,
        out_shape=jax.ShapeDtypeStruct((N_MEM, 2 * MEM_W), BF16),
        compiler_params=pltpu.CompilerParams(vmem_limit_bytes=32 * MIB),
        name="mem_kv",
    )(mem, g_mem.reshape(1, D_MODEL), w_mem_kv)


def _mem_attn_kernel(q_ref, gate_ref, kv_ref, o_ref):
    scale = HEAD_DIM ** -0.5
    for h in range(MEM_HEADS):
        hc = slice(h * HEAD_DIM, (h + 1) * HEAD_DIM)
        km = kv_ref[:, h * HEAD_DIM:(h + 1) * HEAD_DIM]
        vm = kv_ref[:, MEM_W + h * HEAD_DIM:MEM_W + (h + 1) * HEAD_DIM]
        s = lax.dot_general(q_ref[:, hc], km, (((1,), (1,)), ((), ())), preferred_element_type=F32) * scale
        m = jnp.max(s, axis=-1, keepdims=True)
        e = jnp.exp(s - m)
        p = (e / jnp.sum(e, axis=-1, keepdims=True)).astype(BF16)
        o = jnp.dot(p, vm, preferred_element_type=F32)
        gate = gate_ref[:, hc].astype(F32)
        o_ref[:, hc] = (o * _silu(gate)).astype(o_ref.dtype)


def _mem_attn(main, kv, tm=512):
    s = main.shape[0]
    return pl.pallas_call(
        _mem_attn_kernel,
        out_shape=jax.ShapeDtypeStruct((s, MEM_W), BF16),
        grid=(s // tm,),
        in_specs=[pl.BlockSpec((tm, MEM_W), lambda i: (i, CB_MQ)),
                  pl.BlockSpec((tm, MEM_W), lambda i: (i, CB_MGATE)),
                  pl.BlockSpec((N_MEM, 2 * MEM_W), lambda i: (0, 0))],
        out_specs=pl.BlockSpec((tm, MEM_W), lambda i: (i, 0)),
        compiler_params=_params(("parallel",), 32),
        name="mem_attn",
    )(main, main, kv)


def _merge_kernel(ya_ref, yb_ref, yc_ref, ym_ref, gl_ref, bm_ref, wa_ref, wb_ref, wc_ref, wm_ref, wo_ref,
                  gp_ref, x_ref, o_ref):
    merged = None
    for j, (y_ref, w_ref) in enumerate(((ya_ref, wa_ref), (yb_ref, wb_ref), (yc_ref, wc_ref), (ym_ref, wm_ref))):
        br = jnp.dot(y_ref[...], w_ref[...], preferred_element_type=F32)
        gate = _sigmoid(gl_ref[:, j * D_MODEL:(j + 1) * D_MODEL].astype(F32) + bm_ref[j:j + 1, :])
        term = gate * br
        merged = term if merged is None else merged + term
    out = jnp.dot(merged.astype(BF16), wo_ref[...], preferred_element_type=F32)
    ms = jnp.mean(out * out, axis=-1, keepdims=True)
    o_ref[...] = x_ref[...] + out * lax.rsqrt(ms + EPS) * gp_ref[...]


def _merge(ya, yb, yc, ym, glog, b_merge, wa, wb, wc, wm, wo, g_post, x, tm=256):
    s = x.shape[0]
    row = lambda w: pl.BlockSpec((tm, w), lambda i: (i, 0))
    const = lambda a: pl.BlockSpec(a.shape, lambda i: (0,) * a.ndim, pipeline_mode=pl.Buffered(1))
    g_post = g_post.reshape(1, D_MODEL)
    return pl.pallas_call(
        _merge_kernel,
        out_shape=jax.ShapeDtypeStruct((s, D_MODEL), F32),
        grid=(s // tm,),
        in_specs=[row(POOL_W), row(SWA_OUT), row(GLA_V), row(MEM_W), row(N_BRANCH * D_MODEL),
                  const(b_merge), const(wa), const(wb), const(wc), const(wm), const(wo), const(g_post),
                  row(D_MODEL)],
        out_specs=row(D_MODEL),
        compiler_params=_params(("parallel",), 56),
        name="merge_out",
    )(ya, yb, yc, ym, glog, b_merge, wa, wb, wc, wm, wo, g_post, x)


def _layer(layer, x, mem, g_pre, g_post, g_mem, w_in, b_merge, w_pool, pool_scale, w_alpha, b_alpha, g_gla,
           w_mem_kv, w_br_pool, w_br_swa, w_br_gla, w_br_mem, w_out):
    w_alpha_pad = jnp.pad(w_alpha, ((0, LANE - GLA_RANK), (0, 0))).astype(BF16)

    s = x.shape[0]
    dils = tuple(dil for _, dil in SWA_CONFIGS[1:])
    h, h_d1, h_d2 = _rmsnorm(x, g_pre, dils)
    main = _project(h, w_in, layer, MAIN_TILES, tn=PROJ_TN, name="project_main")
    tail = _project(h, w_in, layer, TAIL_TILES, tn=PROJ_TN, shift=TAIL_SHIFT, name="project_tail")
    lr = _project(h, w_in, layer, LR_TILES, tn=LANE, name="project_lr")
    qkv1 = _project(h_d1.reshape(s, D_MODEL), w_in, layer, _swa_tiles(1), tn=PROJ_TN, name="project_dil4")
    qkv2 = _project(h_d2.reshape(s, D_MODEL), w_in, layer, _swa_tiles(2), tn=PROJ_TN, name="project_dil16")

    ya = _pool_mixer(main, w_pool.astype(BF16), pool_scale)
    o0, l0 = _swa_group(main.reshape(1, s, MAIN_W), (CB_Q0, CB_K0, CB_V0), 0)
    o1, l1 = _swa_group(qkv1.reshape(dils[0], s // dils[0], 3 * SWA_OUT), (0, 1, 2), 1)
    o2, l2 = _swa_group(qkv2.reshape(dils[1], s // dils[1], 3 * SWA_OUT), (0, 1, 2), 2)
    yb = _swa_combine((o0, o1, o2), (l0, l1, l2), main)
    yc = _gla_mixer(main, lr, w_alpha_pad, b_alpha, g_gla)
    kv = _mem_kv(mem, g_mem, w_mem_kv.astype(BF16))
    ym = _mem_attn(tail, kv)
    return _merge(ya, yb, yc, ym, tail, b_merge, w_br_pool.astype(BF16), w_br_swa.astype(BF16),
                  w_br_gla.astype(BF16), w_br_mem.astype(BF16), w_out.astype(BF16), g_post, x)


def kernel(x, mem, g_pre, g_post, g_mem, w_in, b_merge, w_pool, pool_scale, w_alpha, b_alpha, g_gla, w_mem_kv,
           w_br_pool, w_br_swa, w_br_gla, w_br_mem, w_out):
    b, s, d = x.shape
    assert (b, s, d) == (1, SEQ, D_MODEL) and mem.shape == (1, N_MEM, D_MODEL)
    xs = x[0]
    for l in range(g_pre.shape[0]):
        xs = _layer(l, xs, mem[0], g_pre[l], g_post[l], g_mem[l], w_in, b_merge[l], w_pool[l], pool_scale[l],
                    w_alpha[l], b_alpha[l], g_gla[l], w_mem_kv[l], w_br_pool[l], w_br_swa[l], w_br_gla[l],
                    w_br_mem[l], w_out[l])
    return xs[None]
```

```python
import functools

import jax
import jax.numpy as jnp
from jax import lax
from jax.experimental import pallas as pl
from jax.experimental.pallas import tpu as pltpu

F32 = jnp.float32
BF16 = jnp.bfloat16

D_MODEL = 2048
SEQ = 16384
N_MEM = 256
EPS = 1e-6
NEG = -1e30
POOL_WINDOWS = (2, 4, 8, 16)
POOL_GW = 256
POOL_W = 1024
POOL_HALO = 16
SWA_CONFIGS = ((128, 1), (512, 4), (2048, 16))
SWA_HPG = 4
SWA_HEADS = 12
HEAD_DIM = 128
SWA_OUT = 512
BLK = 128
GLA_HEADS = 4
GLA_DK = 128
GLA_DV = 256
GLA_K = 512
GLA_V = 1024
GLA_RANK = 16
GLA_TAU = 16.0
GLA_CHUNK = 64
MEM_HEADS = 4
MEM_W = 512
N_BRANCH = 4
LANE = 128

OFF_A_VAL, OFF_A_GATE = 0, 1024
OFF_SQ, OFF_SK, OFF_SV, OFF_SGATE = 2048, 3584, 5120, 6656
OFF_CQ, OFF_CK, OFF_CV, OFF_CGATE, OFF_CLR = 7168, 7680, 8192, 9216, 10240
OFF_MQ, OFF_MGATE, OFF_GLOG = 10256, 10768, 11280

PROJ_TN = 512
SWA_STEP = 2048
SWA_UNROLL = 4


def _tiles(off, width, tn=PROJ_TN, shift=0):
    assert (off - shift) % tn == 0 and width % tn == 0
    return tuple(range((off - shift) // tn, (off - shift + width) // tn))


def _swa_tiles(group):
    return tuple(t for off in (OFF_SQ, OFF_SK, OFF_SV) for t in _tiles(off + group * SWA_OUT, SWA_OUT))


MAIN_TILES = (_tiles(OFF_A_VAL, 2 * POOL_W) + _swa_tiles(0) + _tiles(OFF_SGATE, SWA_OUT)
              + _tiles(OFF_CQ, 2 * GLA_K + 2 * GLA_V))
MAIN_W = len(MAIN_TILES) * PROJ_TN
CB_AVAL, CB_AGATE = 0, 2
CB_Q0, CB_K0, CB_V0, CB_SGATE = 4, 5, 6, 7
CB_CQ, CB_CK, CB_CV, CB_CGATE = 8, 9, 10, 12
TAIL_SHIFT = OFF_MQ % LANE
TAIL_TILES = (_tiles(OFF_GLOG, N_BRANCH * D_MODEL, shift=TAIL_SHIFT) + _tiles(OFF_MQ, 2 * MEM_W, shift=TAIL_SHIFT))
CB_GLOG, CB_MQ, CB_MGATE = 0, 16, 17
LR_TILES = _tiles(OFF_CLR, LANE, tn=LANE)

MIB = 1024 * 1024


def _params(sem, vmem_mib):
    return pltpu.CompilerParams(dimension_semantics=sem, vmem_limit_bytes=vmem_mib * MIB)


def _sigmoid(x):
    return 1.0 / (1.0 + jnp.exp(-x))


def _silu(x):
    return x * _sigmoid(x)


def _rmsnorm_kernel(x_ref, g_ref, h_ref, *rest, dils, tm):
    nd = len(dils)
    dil_refs, slab_refs = rest[:nd], rest[nd:]
    x = x_ref[...]
    ms = jnp.mean(x * x, axis=-1, keepdims=True)
    hf = x * lax.rsqrt(ms + EPS) * g_ref[...]
    h_ref[...] = hf.astype(h_ref.dtype)
    nslab = hf.shape[1] // LANE
    for c in range(nslab):
        slab_refs[0][c] = hf[:, c * LANE:(c + 1) * LANE]
    prev = 1
    for k, (ref, dil) in enumerate(zip(dil_refs, dils)):
        ratio, n = dil // prev, tm // dil
        for g in range(prev):
            for q in range(ratio):
                r = g + prev * q
                rows = pl.ds(g * (tm // prev) + q, n, stride=ratio)
                pieces = [slab_refs[k][c, rows, :] for c in range(nslab)]
                ref[r] = jnp.concatenate(pieces, axis=1).astype(ref.dtype)
                if k + 1 < nd:
                    for c in range(nslab):
                        slab_refs[k + 1][c, r * n:(r + 1) * n, :] = pieces[c]
        prev = dil


def _rmsnorm(x, g, dils, tm=512):
    s, d = x.shape
    assert all(b % a == 0 for a, b in zip((1,) + dils, dils))
    outs = [jax.ShapeDtypeStruct((s, d), BF16)] + [jax.ShapeDtypeStruct((dil, s // dil, d), BF16) for dil in dils]
    out_specs = [pl.BlockSpec((tm, d), lambda i: (i, 0))] + [
        pl.BlockSpec((dil, tm // dil, d), lambda i: (0, i, 0)) for dil in dils]
    return pl.pallas_call(
        functools.partial(_rmsnorm_kernel, dils=dils, tm=tm),
        out_shape=outs,
        grid=(s // tm,),
        in_specs=[pl.BlockSpec((tm, d), lambda i: (i, 0)),
                  pl.BlockSpec((1, d), lambda i: (0, 0))],
        out_specs=out_specs,
        scratch_shapes=[pltpu.VMEM((d // LANE, tm, LANE), F32) for _ in dils],
        compiler_params=_params(("parallel",), 40),
        name="rmsnorm",
    )(x, g.reshape(1, d))


def _project_kernel(tiles_ref, h_ref, *rest, shift, wblk, nw):
    del tiles_ref
    per = 2 if shift else 1
    w_refs, (o_ref, wb_ref) = rest[:nw * per], rest[nw * per:]

    @pl.when(pl.program_id(1) == 0)
    def _():
        for k in range(nw):
            lo = k * wblk
            if shift:
                w_ref, wx_ref = w_refs[2 * k], w_refs[2 * k + 1]
                wb_ref[lo:lo + wblk - shift, :] = w_ref[shift:wblk, :].astype(wb_ref.dtype)
                wb_ref[lo + wblk - shift:lo + wblk, :] = wx_ref[...].astype(wb_ref.dtype)
            else:
                wb_ref[lo:lo + wblk, :] = w_refs[k][...].astype(wb_ref.dtype)

    o_ref[...] = lax.dot_general(h_ref[...], wb_ref[...], (((1,), (1,)), ((), ())),
                                 preferred_element_type=F32).astype(o_ref.dtype)


def _project(h, w_t, layer, blocks, *, wblk=PROJ_TN, nw=1, shift=0, tm=1024, name="project"):
    s, d = h.shape
    assert len(blocks) % nw == 0
    tn = nw * wblk
    nj, ni = len(blocks) // nw, s // tm
    in_specs = [pl.BlockSpec((tm, d), lambda j, i, t: (i, 0))]
    for k in range(nw):
        in_specs.append(pl.BlockSpec((None, wblk, d), lambda j, i, t, k=k: (layer, t[j * nw + k], 0)))
        if shift:
            assert shift % 16 == 0 and wblk % shift == 0
            in_specs.append(pl.BlockSpec(
                (None, shift, d), lambda j, i, t, k=k: (layer, (t[j * nw + k] + 1) * (wblk // shift), 0)))
    return pl.pallas_call(
        functools.partial(_project_kernel, shift=shift, wblk=wblk, nw=nw),
        out_shape=jax.ShapeDtypeStruct((s, nj * tn), BF16),
        grid_spec=pltpu.PrefetchScalarGridSpec(
            num_scalar_prefetch=1,
            grid=(nj, ni),
            in_specs=in_specs,
            out_specs=pl.BlockSpec((tm, tn), lambda j, i, t: (i, j)),
            scratch_shapes=[pltpu.VMEM((tn, d), BF16)]),
        compiler_params=_params(("arbitrary", "arbitrary"), 48),
        name=name,
    )(jnp.asarray(blocks, jnp.int32), h, *([w_t] * (len(in_specs) - 1)))


def _pool_kernel(u_ref, halo_ref, gate_ref, w_ref, sc_ref, o_ref, *, tp):
    i = pl.program_id(0)
    u = u_ref[...].astype(F32)
    halo = jnp.where(i == 0, 0.0, halo_ref[...].astype(F32))
    full = jnp.concatenate([halo, u], axis=0)
    pos = i * tp + lax.broadcasted_iota(jnp.int32, (tp, 1), 0)
    for g, w in enumerate(POOL_WINDOWS):
        cols = slice(g * POOL_GW, (g + 1) * POOL_GW)
        s = full[:, cols]
        k = 1
        while k < w:
            s = s + pltpu.roll(s, k, axis=0)
            k *= 2
        s = s[POOL_HALO:]
        inv_cnt = 1.0 / jnp.minimum(pos + 1, w).astype(F32)
        pooled = s * inv_cnt - u[:, cols]
        mixed = jnp.dot(pooled.astype(BF16), w_ref[g], preferred_element_type=F32) * sc_ref[:, cols]
        gate = gate_ref[:, cols].astype(F32)
        o_ref[:, cols] = (mixed * _silu(gate)).astype(o_ref.dtype)


def _pool_mixer(main, w_pool, pool_scale, tp=512):
    s = main.shape[0]
    hb = tp // POOL_HALO
    return pl.pallas_call(
        functools.partial(_pool_kernel, tp=tp),
        out_shape=jax.ShapeDtypeStruct((s, POOL_W), BF16),
        grid=(s // tp,),
        in_specs=[pl.BlockSpec((tp, POOL_W), lambda i: (i, CB_AVAL // 2)),
                  pl.BlockSpec((POOL_HALO, POOL_W), lambda i: (jnp.maximum(i * hb - 1, 0), CB_AVAL // 2)),
                  pl.BlockSpec((tp, POOL_W), lambda i: (i, CB_AGATE // 2)),
                  pl.BlockSpec((4, POOL_GW, POOL_GW), lambda i: (0, 0, 0)),
                  pl.BlockSpec((1, POOL_W), lambda i: (0, 0))],
        out_specs=pl.BlockSpec((tp, POOL_W), lambda i: (i, 0)),
        compiler_params=_params(("parallel",), 32),
        name="pool_mixer",
    )(main, main, main, w_pool, pool_scale.reshape(1, POOL_W))


def _swa_kernel(q_ref, kc_ref, kp_ref, vc_ref, vp_ref, o_ref, lse_ref, *, dil, slopes, nb):
    n = pl.program_id(0)
    qi = lax.broadcasted_iota(jnp.int32, (BLK, 2 * BLK), 0)
    kj = lax.broadcasted_iota(jnp.int32, (BLK, 2 * BLK), 1)
    delta = qi + BLK - kj
    in_window = (delta >= 0) & (delta <= BLK)
    dist = (delta * dil).astype(F32)
    lane = lax.broadcasted_iota(jnp.int32, (BLK, LANE), 1)
    scale = HEAD_DIM ** -0.5

    def unit(u, carry):
        r = lax.shift_right_logical(u, nb.bit_length() - 1)
        b = u & (nb - 1)
        row0 = pl.multiple_of(b * BLK, BLK)
        prow0 = pl.multiple_of(jnp.maximum(b - 1, 0) * BLK, BLK)
        first = b == 0
        k_prev = jnp.where(first, kp_ref[r], kc_ref[r, pl.ds(prow0, BLK), :])
        v_prev = jnp.where(first, vp_ref[r], vc_ref[r, pl.ds(prow0, BLK), :])
        k_blk = jnp.concatenate([k_prev, kc_ref[r, pl.ds(row0, BLK), :]], axis=0)
        v_blk = jnp.concatenate([v_prev, vc_ref[r, pl.ds(row0, BLK), :]], axis=0)
        q_blk = q_ref[r, pl.ds(row0, BLK), :]
        valid = in_window & jnp.logical_not(((n == 0) & first) & (kj < BLK))
        rows = pl.ds(row0, BLK) if dil == 1 else pl.ds(b * (BLK * dil) + r, BLK, stride=dil)
        lse_acc = jnp.zeros((BLK, LANE), F32)
        for h in range(SWA_HPG):
            hc = slice(h * HEAD_DIM, (h + 1) * HEAD_DIM)
            s = lax.dot_general(q_blk[:, hc], k_blk[:, hc], (((1,), (1,)), ((), ())),
                                preferred_element_type=F32) * scale
            s = jnp.where(valid, s - slopes[h] * dist, NEG)
            m = jnp.max(s, axis=-1, keepdims=True)
            e = jnp.exp(s - m)
            den = jnp.sum(e, axis=-1, keepdims=True)
            p = (e / den).astype(BF16)
            o_ref[h, rows, :] = jnp.dot(p, v_blk[:, hc], preferred_element_type=F32)
            lse_acc = jnp.where(lane == h, m + jnp.log(den), lse_acc)
        lse_ref[rows, :] = lse_acc
        return carry

    lax.fori_loop(0, dil * nb, unit, 0, unroll=SWA_UNROLL)


def _swa_group(qkv, cols, group):
    window, dil = SWA_CONFIGS[group]
    sub = qkv.shape[1]
    s = sub * dil
    tq = SWA_STEP // dil
    nb = tq // BLK
    qc, kc, vc = cols
    slopes = tuple(float(2.0 ** (-8.0 * (group * SWA_HPG + h + 1.0) / SWA_HEADS)) for h in range(SWA_HPG))
    cur = lambda c: pl.BlockSpec((dil, tq, SWA_OUT), lambda n: (0, n, c))
    prev = lambda c: pl.BlockSpec((dil, BLK, SWA_OUT), lambda n: (0, jnp.maximum(n * nb - 1, 0), c))
    return pl.pallas_call(
        functools.partial(_swa_kernel, dil=dil, slopes=slopes, nb=nb),
        out_shape=(jax.ShapeDtypeStruct((SWA_HPG, s, HEAD_DIM), F32),
                   jax.ShapeDtypeStruct((s, LANE), F32)),
        grid=(s // SWA_STEP,),
        in_specs=[cur(qc), cur(kc), prev(kc), cur(vc), prev(vc)],
        out_specs=(pl.BlockSpec((SWA_HPG, SWA_STEP, HEAD_DIM), lambda n: (0, n, 0)),
                   pl.BlockSpec((SWA_STEP, LANE), lambda n: (n, 0))),
        compiler_params=_params(("parallel",), 48),
        name=f"swa_group{group}",
    )(qkv, qkv, qkv, qkv, qkv)


def _swa_combine_kernel(o0, o1, o2, l0, l1, l2, gate_ref, y_ref):
    ls = [l0[...], l1[...], l2[...]]
    mx = jnp.maximum(jnp.maximum(ls[0], ls[1]), ls[2])
    es = [jnp.exp(l - mx) for l in ls]
    den = es[0] + es[1] + es[2]
    ws = [e / den for e in es]
    os_ = [o0, o1, o2]
    for h in range(SWA_HPG):
        hc = slice(h * HEAD_DIM, (h + 1) * HEAD_DIM)
        acc = None
        for g in range(3):
            term = ws[g][:, h:h + 1] * os_[g][h]
            acc = term if acc is None else acc + term
        gate = gate_ref[:, hc].astype(F32)
        y_ref[:, hc] = (acc * _silu(gate)).astype(y_ref.dtype)


def _swa_combine(os_, lses, main, tm=512):
    s = main.shape[0]
    ospec = pl.BlockSpec((SWA_HPG, tm, HEAD_DIM), lambda i: (0, i, 0))
    lspec = pl.BlockSpec((tm, LANE), lambda i: (i, 0))
    return pl.pallas_call(
        _swa_combine_kernel,
        out_shape=jax.ShapeDtypeStruct((s, SWA_OUT), BF16),
        grid=(s // tm,),
        in_specs=[ospec, ospec, ospec, lspec, lspec, lspec,
                  pl.BlockSpec((tm, SWA_OUT), lambda i: (i, CB_SGATE))],
        out_specs=pl.BlockSpec((tm, SWA_OUT), lambda i: (i, 0)),
        compiler_params=_params(("parallel",), 32),
        name="swa_combine",
    )(*os_, *lses, main)


def _gla_kernel(q_ref, k_ref, v_ref, gate_ref, lr_ref, wa_ref, ba_ref, gg_ref, o_ref, state_ref, *, tt):
    @pl.when(pl.program_id(0) == 0)
    def _():
        state_ref[...] = jnp.zeros_like(state_ref)

    c = GLA_CHUNK
    z = jnp.dot(lr_ref[...], wa_ref[...], preferred_element_type=F32) + ba_ref[...]
    log_a = (jnp.minimum(z, 0.0) - jnp.log(1.0 + jnp.exp(-jnp.abs(z)))) / GLA_TAU
    in_chunk = lax.broadcasted_iota(jnp.int32, (tt, GLA_K), 0) & (c - 1)
    bc = log_a
    k = 1
    while k < c:
        bc = bc + jnp.where(in_chunk >= k, pltpu.roll(bc, k, axis=0), 0.0)
        k *= 2
    q_t = (q_ref[...].astype(F32) * (GLA_DK ** -0.5)) * jnp.exp(bc)
    kf = k_ref[...].astype(F32)
    k_t = kf * jnp.exp(-bc)
    q_tb = q_t.astype(BF16)
    k_tb = k_t.astype(BF16)
    ri = lax.broadcasted_iota(jnp.int32, (tt, tt), 0)
    ci = lax.broadcasted_iota(jnp.int32, (tt, tt), 1)
    causal = ((ri & -c) == (ci & -c)) & (ci <= ri)
    nchunk = tt // c
    b_last = [bc[(j + 1) * c - 1:(j + 1) * c, :] for j in range(nchunk)]
    k_dec = jnp.concatenate(
        [kf[j * c:(j + 1) * c, :] * jnp.exp(b_last[j] - bc[j * c:(j + 1) * c, :]) for j in range(nchunk)],
        axis=0).astype(BF16)
    decay = [jnp.exp(b) for b in b_last]
    for h in range(GLA_HEADS):
        kc = slice(h * GLA_DK, (h + 1) * GLA_DK)
        vc = slice(h * GLA_DV, (h + 1) * GLA_DV)
        v_h = v_ref[:, vc]
        a = lax.dot_general(q_tb[:, kc], k_tb[:, kc], (((1,), (1,)), ((), ())), preferred_element_type=F32)
        a = jnp.where(causal, a, 0.0)
        o = jnp.dot(a.astype(BF16), v_h, preferred_element_type=F32)
        state = state_ref[h]
        inter = []
        for j in range(nchunk):
            rows = slice(j * c, (j + 1) * c)
            inter.append(lax.dot_general(q_tb[rows, kc], state.astype(BF16), (((1,), (1,)), ((), ())),
                                         preferred_element_type=F32))
            kv_t = lax.dot_general(v_h[rows, :], k_dec[rows, kc], (((0,), (0,)), ((), ())),
                                   preferred_element_type=F32)
            state = decay[j][:, kc] * state + kv_t
        state_ref[h] = state
        o = o + jnp.concatenate(inter, axis=0)
        o = o * lax.rsqrt(jnp.mean(o * o, axis=-1, keepdims=True) + EPS)
        o = o * gg_ref[:, vc]
        gate = gate_ref[:, vc].astype(F32)
        o_ref[:, vc] = (o * _silu(gate)).astype(o_ref.dtype)


def _gla_mixer(main, lr, w_alpha_pad, b_alpha, g_gla, tt=256):
    s = main.shape[0]
    return pl.pallas_call(
        functools.partial(_gla_kernel, tt=tt),
        out_shape=jax.ShapeDtypeStruct((s, GLA_V), BF16),
        grid=(s // tt,),
        in_specs=[pl.BlockSpec((tt, GLA_K), lambda i: (i, CB_CQ)),
                  pl.BlockSpec((tt, GLA_K), lambda i: (i, CB_CK)),
                  pl.BlockSpec((tt, GLA_V), lambda i: (i, CB_CV // 2)),
                  pl.BlockSpec((tt, GLA_V), lambda i: (i, CB_CGATE // 2)),
                  pl.BlockSpec((tt, LANE), lambda i: (i, 0)),
                  pl.BlockSpec((LANE, GLA_K), lambda i: (0, 0)),
                  pl.BlockSpec((1, GLA_K), lambda i: (0, 0)),
                  pl.BlockSpec((1, GLA_V), lambda i: (0, 0))],
        out_specs=pl.BlockSpec((tt, GLA_V), lambda i: (i, 0)),
        scratch_shapes=[pltpu.VMEM((GLA_HEADS, GLA_DV, GLA_DK), F32)],
        compiler_params=_params(("arbitrary",), 32),
        name="gla_mixer",
    )(main, main, main, main, lr, w_alpha_pad, b_alpha.reshape(1, GLA_K), g_gla.reshape(1, GLA_V))


def _mem_kv_kernel(mem_ref, g_ref, w_ref, o_ref):
    x = mem_ref[...]
    ms = jnp.mean(x * x, axis=-1, keepdims=True)
    hm = (x * lax.rsqrt(ms + EPS) * g_ref[...]).astype(BF16)
    o_ref[...] = jnp.dot(hm, w_ref[...], preferred_element_type=F32).astype(o_ref.dtype)


def _mem_kv(mem, g_mem, w_mem_kv):
    return pl.pallas_call(
        _mem_kv_kernel,
        out_shape=jax.ShapeDtypeStruct((N_MEM, 2 * MEM_W), BF16),
        compiler_params=pltpu.CompilerParams(vmem_limit_bytes=32 * MIB),
        name="mem_kv",
    )(mem, g_mem.reshape(1, D_MODEL), w_mem_kv)


def _mem_attn_kernel(q_ref, gate_ref, kv_ref, o_ref):
    scale = HEAD_DIM ** -0.5
    for h in range(MEM_HEADS):
        hc = slice(h * HEAD_DIM, (h + 1) * HEAD_DIM)
        km = kv_ref[:, h * HEAD_DIM:(h + 1) * HEAD_DIM]
        vm = kv_ref[:, MEM_W + h * HEAD_DIM:MEM_W + (h + 1) * HEAD_DIM]
        s = lax.dot_general(q_ref[:, hc], km, (((1,), (1,)), ((), ())), preferred_element_type=F32) * scale
        m = jnp.max(s, axis=-1, keepdims=True)
        e = jnp.exp(s - m)
        p = (e / jnp.sum(e, axis=-1, keepdims=True)).astype(BF16)
        o = jnp.dot(p, vm, preferred_element_type=F32)
        gate = gate_ref[:, hc].astype(F32)
        o_ref[:, hc] = (o * _silu(gate)).astype(o_ref.dtype)


def _mem_attn(main, kv, tm=512):
    s = main.shape[0]
    return pl.pallas_call(
        _mem_attn_kernel,
        out_shape=jax.ShapeDtypeStruct((s, MEM_W), BF16),
        grid=(s // tm,),
        in_specs=[pl.BlockSpec((tm, MEM_W), lambda i: (i, CB_MQ)),
                  pl.BlockSpec((tm, MEM_W), lambda i: (i, CB_MGATE)),
                  pl.BlockSpec((N_MEM, 2 * MEM_W), lambda i: (0, 0))],
        out_specs=pl.BlockSpec((tm, MEM_W), lambda i: (i, 0)),
        compiler_params=_params(("parallel",), 32),
        name="mem_attn",
    )(main, main, kv)


def _merge_kernel(ya_ref, yb_ref, yc_ref, ym_ref, gl_ref, bm_ref, wa_ref, wb_ref, wc_ref, wm_ref, wo_ref,
                  gp_ref, x_ref, o_ref):
    merged = None
    for j, (y_ref, w_ref) in enumerate(((ya_ref, wa_ref), (yb_ref, wb_ref), (yc_ref, wc_ref), (ym_ref, wm_ref))):
        br = jnp.dot(y_ref[...], w_ref[...], preferred_element_type=F32)
        gate = _sigmoid(gl_ref[:, j * D_MODEL:(j + 1) * D_MODEL].astype(F32) + bm_ref[j:j + 1, :])
        term = gate * br
        merged = term if merged is None else merged + term
    out = jnp.dot(merged.astype(BF16), wo_ref[...], preferred_element_type=F32)
    ms = jnp.mean(out * out, axis=-1, keepdims=True)
    o_ref[...] = x_ref[...] + out * lax.rsqrt(ms + EPS) * gp_ref[...]


def _merge(ya, yb, yc, ym, glog, b_merge, wa, wb, wc, wm, wo, g_post, x, tm=256):
    s = x.shape[0]
    row = lambda w: pl.BlockSpec((tm, w), lambda i: (i, 0))
    const = lambda a: pl.BlockSpec(a.shape, lambda i: (0,) * a.ndim, pipeline_mode=pl.Buffered(1))
    g_post = g_post.reshape(1, D_MODEL)
    return pl.pallas_call(
        _merge_kernel,
        out_shape=jax.ShapeDtypeStruct((s, D_MODEL), F32),
        grid=(s // tm,),
        in_specs=[row(POOL_W), row(SWA_OUT), row(GLA_V), row(MEM_W), row(N_BRANCH * D_MODEL),
                  const(b_merge), const(wa), const(wb), const(wc), const(wm), const(wo), const(g_post),
                  row(D_MODEL)],
        out_specs=row(D_MODEL),
        compiler_params=_params(("parallel",), 56),
        name="merge_out",
    )(ya, yb, yc, ym, glog, b_merge, wa, wb, wc, wm, wo, g_post, x)


def _layer(layer, x, mem, g_pre, g_post, g_mem, w_in, b_merge, w_pool, pool_scale, w_alpha, b_alpha, g_gla,
           w_mem_kv, w_br_pool, w_br_swa, w_br_gla, w_br_mem, w_out):
    w_alpha_pad = jnp.pad(w_alpha, ((0, LANE - GLA_RANK), (0, 0))).astype(BF16)

    s = x.shape[0]
    dils = tuple(dil for _, dil in SWA_CONFIGS[1:])
    h, h_d1, h_d2 = _rmsnorm(x, g_pre, dils)
    main = _project(h, w_in, layer, MAIN_TILES, nw=2, name="project_main")
    tail = _project(h, w_in, layer, TAIL_TILES, nw=2, shift=TAIL_SHIFT, name="project_tail")
    lr = _project(h, w_in, layer, LR_TILES, wblk=LANE, tm=2048, name="project_lr")
    qkv1 = _project(h_d1.reshape(s, D_MODEL), w_in, layer, _swa_tiles(1), tm=2048, name="project_dil4")
    qkv2 = _project(h_d2.reshape(s, D_MODEL), w_in, layer, _swa_tiles(2), tm=2048, name="project_dil16")

    ya = _pool_mixer(main, w_pool.astype(BF16), pool_scale)
    o0, l0 = _swa_group(main.reshape(1, s, MAIN_W), (CB_Q0, CB_K0, CB_V0), 0)
    o1, l1 = _swa_group(qkv1.reshape(dils[0], s // dils[0], 3 * SWA_OUT), (0, 1, 2), 1)
    o2, l2 = _swa_group(qkv2.reshape(dils[1], s // dils[1], 3 * SWA_OUT), (0, 1, 2), 2)
    yb = _swa_combine((o0, o1, o2), (l0, l1, l2), main)
    yc = _gla_mixer(main, lr, w_alpha_pad, b_alpha, g_gla)
    kv = _mem_kv(mem, g_mem, w_mem_kv.astype(BF16))
    ym = _mem_attn(tail, kv)
    return _merge(ya, yb, yc, ym, tail, b_merge, w_br_pool.astype(BF16), w_br_swa.astype(BF16),
                  w_br_gla.astype(BF16), w_br_mem.astype(BF16), w_out.astype(BF16), g_post, x)


def kernel(x, mem, g_pre, g_post, g_mem, w_in, b_merge, w_pool, pool_scale, w_alpha, b_alpha, g_gla, w_mem_kv,
           w_br_pool, w_br_swa, w_br_gla, w_br_mem, w_out):
    b, s, d = x.shape
    assert (b, s, d) == (1, SEQ, D_MODEL) and mem.shape == (1, N_MEM, D_MODEL)
    xs = x[0]
    w_t = jnp.transpose(w_in, (0, 2, 1))
    for l in range(g_pre.shape[0]):
        xs = _layer(l, xs, mem[0], g_pre[l], g_post[l], g_mem[l], w_t, b_merge[l], w_pool[l], pool_scale[l],
                    w_alpha[l], b_alpha[l], g_gla[l], w_mem_kv[l], w_br_pool[l], w_br_swa[l], w_br_gla[l],
                    w_br_mem[l], w_out[l])
    return xs[None]
```
